```python
import jax, jax.numpy as jnp
from jax import lax
import numpy as np

D_MODEL = 1024
BATCH = 4
SEQ = 8192
DEPTH = 2
DEC_BATCH = 32
DEC_SEQ = 4
PAST_LEN = 16384
PAGE_SIZE = 128

N_A_LAYERS = DEPTH // 2
N_B_LAYERS = DEPTH - N_A_LAYERS
N_DENSE = (DEPTH + 1) // 2
N_MOE = DEPTH // 2

CONV_WIDTH = 31
D_CONV = D_MODEL

HEAD_DIM = 64
GROUPS = ((128, 1), (512, 4), (2048, 16))
N_GROUPS = len(GROUPS)
HEADS_PER_GROUP = 4
N_HEADS = N_GROUPS * HEADS_PER_GROUP
D_ATTN = N_HEADS * HEAD_DIM
ROT_DIM = HEAD_DIM // 4
ROPE_THETA = 500000.0

D_FF = 2816
N_EXPERTS = 8
TOP_K = 2
D_EXPERT = 3584

EPS = 1e-5

kernel_name = "yoco_conformer_dilated_swa_step"


def rms_norm(x, g):
    xf = x.astype(jnp.float32)
    y = xf * lax.rsqrt(jnp.mean(xf * xf, axis=-1, keepdims=True) + EPS)
    return (y * g.astype(jnp.float32)).astype(x.dtype)


def layer_norm(x, g, b):
    xf = x.astype(jnp.float32)
    mu = jnp.mean(xf, axis=-1, keepdims=True)
    xc = xf - mu
    y = xc * lax.rsqrt(jnp.mean(xc * xc, axis=-1, keepdims=True) + EPS)
    return (y * g.astype(jnp.float32) + b.astype(jnp.float32)).astype(x.dtype)


def rope_partial(x, pos):
    half = ROT_DIM // 2
    inv_freq = ROPE_THETA ** (-jnp.arange(half, dtype=jnp.float32) / half)
    ang = pos.astype(jnp.float32)[:, None] * inv_freq[None, :]
    cos = jnp.cos(ang)[None, :, None, :]
    sin = jnp.sin(ang)[None, :, None, :]
    xr = x[..., :ROT_DIM].astype(jnp.float32)
    x1, x2 = xr[..., :half], xr[..., half:]
    rot = jnp.concatenate([x1 * cos - x2 * sin, x2 * cos + x1 * sin], axis=-1)
    return jnp.concatenate([rot.astype(x.dtype), x[..., ROT_DIM:]], axis=-1)


def conv_module(hn, conv_state, w_pw1, b_pw1, w_dw, b_dw, ln_g, ln_b, w_pw2, b_pw2):
    u = hn @ w_pw1 + b_pw1
    a, gate = jnp.split(u, 2, axis=-1)
    g = a * jax.nn.sigmoid(gate)
    full = jnp.concatenate([conv_state.astype(g.dtype), g], axis=1)
    c = lax.conv_general_dilated(
        full, w_dw[:, None, :], window_strides=(1,), padding="VALID",
        dimension_numbers=("NWC", "WIO", "NWC"), feature_group_count=D_CONV) + b_dw
    c = jax.nn.silu(layer_norm(c, ln_g, ln_b))
    out = c @ w_pw2 + b_pw2
    return out, full[:, -(CONV_WIDTH - 1):]


def swiglu(x, wg, wu, wd):
    return (jax.nn.silu(x @ wg) * (x @ wu)) @ wd


def moe_swiglu(x, w_router, wg, wu, wd):
    logits = (x @ w_router).astype(jnp.float32)
    top_v, top_i = lax.top_k(logits, TOP_K)
    gates = jax.nn.softmax(top_v, axis=-1)
    dense_gate = jnp.sum(jax.nn.one_hot(top_i, N_EXPERTS, dtype=jnp.float32)
                         * gates[..., None], axis=-2)
    out = jnp.zeros_like(x)
    for e in range(N_EXPERTS):
        out = out + dense_gate[..., e:e + 1].astype(x.dtype) * swiglu(x, wg[e], wu[e], wd[e])
    return out


def dilated_band_attention(q, k, v, dil, n):
    B, S, H, Dh = q.shape
    L = S // dil
    nb = -(-L // n)
    Lp = nb * n
    scale = HEAD_DIM ** -0.5

    def to_res(x):
        x = x.reshape(B, L, dil, H, Dh).transpose(0, 2, 1, 3, 4)
        return jnp.pad(x, ((0, 0), (0, 0), (0, Lp - L), (0, 0), (0, 0)))

    def key_windows(x):
        xp = jnp.pad(to_res(x), ((0, 0), (0, 0), (n, 0), (0, 0), (0, 0)))
        xp = xp.reshape(B, dil, nb + 1, n, H, Dh)
        return jnp.concatenate([xp[:, :, :-1], xp[:, :, 1:]], axis=3)

    qr = to_res(q).reshape(B, dil, nb, n, H, Dh)
    kw, vw = key_windows(k), key_windows(v)
    s = jnp.einsum("brnqhd,brnkhd->brnhqk", qr, kw,
                   preferred_element_type=jnp.float32) * scale
    rel = (jnp.arange(n)[:, None] + n) - jnp.arange(2 * n)[None, :]
    key_idx = jnp.arange(nb)[:, None] * n - n + jnp.arange(2 * n)[None, :]
    mask = ((rel >= 0) & (rel <= n))[None, :, :] & (key_idx >= 0)[:, None, :]
    s = jnp.where(mask[None, None, :, None], s, -jnp.inf)
    lse = jax.nn.logsumexp(s, axis=-1)
    p = jnp.exp(s - lse[..., None])
    o = jnp.einsum("brnhqk,brnkhd->brnqhd", p.astype(v.dtype), vw)
    o = o.reshape(B, dil, Lp, H, Dh)[:, :, :L].transpose(0, 2, 1, 3, 4).reshape(B, S, H, Dh)
    lse = lse.transpose(0, 1, 2, 4, 3).reshape(B, dil, Lp, H)[:, :, :L]
    lse = lse.transpose(0, 2, 1, 3).reshape(B, S, H)
    return o, lse


def dilated_cached_attention(q, k_all, v_all, dil, n, buf_len, past_len):
    T = q.shape[1]
    scale = HEAD_DIM ** -0.5
    pos_q = past_len + jnp.arange(T)
    pos_k = pos_q[:, None] - dil * jnp.arange(n + 1)[None, :]
    valid = pos_k >= 0
    idx = jnp.clip(pos_k - (past_len - buf_len), 0, buf_len + T - 1)
    kg = k_all[:, idx]
    vg = v_all[:, idx]
    s = jnp.einsum("bthd,btkhd->bthk", q, kg, preferred_element_type=jnp.float32) * scale
    s = jnp.where(valid[None, :, None, :], s, -jnp.inf)
    lse = jax.nn.logsumexp(s, axis=-1)
    p = jnp.exp(s - lse[..., None])
    o = jnp.einsum("bthk,btkhd->bthd", p.astype(vg.dtype), vg)
    return o, lse


def combine_groups(outs, lses):
    lse = jnp.stack(lses, axis=0)
    alpha = jax.nn.softmax(lse, axis=0)
    o = jnp.stack(outs, axis=0) * alpha[..., None].astype(outs[0].dtype)
    G, B, T, Hg, Dh = o.shape
    return jnp.moveaxis(o, 0, 2).reshape(B, T, G * Hg * Dh)


def project_shared_kv(h, norm_kv, w_kv, pos):
    B, T, _ = h.shape
    kv = (rms_norm(h, norm_kv) @ w_kv).reshape(B, T, 2, N_HEADS, HEAD_DIM)
    return rope_partial(kv[:, :, 0], pos), kv[:, :, 1]


def project_q(hn, w_q_l, pos):
    B, T, _ = hn.shape
    return rope_partial((hn @ w_q_l).reshape(B, T, N_HEADS, HEAD_DIM), pos)


def setup_inputs(seed: int = 0) -> dict:
    key = jax.random.key(seed)
    ks = iter(jax.random.split(key, 64))

    def nrm(shape, scale):
        return jax.random.normal(next(ks), shape, jnp.float32) * scale

    buf = [min(w, PAST_LEN) for w, _ in GROUPS]
    kv_shape = lambda L: (DEC_BATCH, L, 2, HEADS_PER_GROUP, HEAD_DIM)
    return {
        "x_prompt": nrm((BATCH, SEQ, D_MODEL), 1.0),
        "x_sample": nrm((DEC_BATCH, DEC_SEQ, D_MODEL), 1.0),
        "state_conv": nrm((N_A_LAYERS, DEC_BATCH, CONV_WIDTH - 1, D_CONV), 0.5),
        "cache_kv_g0": nrm(kv_shape(buf[0]), 1.0),
        "cache_kv_g1": nrm(kv_shape(buf[1]), 1.0),
        "cache_kv_g2": nrm(kv_shape(buf[2]), 1.0),
        "norm_mix": 1.0 + nrm((DEPTH, D_MODEL), 0.02),
        "norm_ffn": 1.0 + nrm((DEPTH, D_MODEL), 0.02),
        "norm_kv": 1.0 + nrm((D_MODEL,), 0.02),
        "norm_final": 1.0 + nrm((D_MODEL,), 0.02),
        "w_pw1": nrm((N_A_LAYERS, D_MODEL, 2 * D_CONV), D_MODEL ** -0.5),
        "b_pw1": nrm((N_A_LAYERS, 2 * D_CONV), 0.02),
        "w_dw": nrm((N_A_LAYERS, CONV_WIDTH, D_CONV), CONV_WIDTH ** -0.5),
        "b_dw": nrm((N_A_LAYERS, D_CONV), 0.02),
        "ln_conv_g": 1.0 + nrm((N_A_LAYERS, D_CONV), 0.02),
        "ln_conv_b": nrm((N_A_LAYERS, D_CONV), 0.02),
        "w_pw2": nrm((N_A_LAYERS, D_CONV, D_MODEL), D_CONV ** -0.5),
        "b_pw2": nrm((N_A_LAYERS, D_MODEL), 0.02),
        "w_q": nrm((N_B_LAYERS, D_MODEL, D_ATTN), D_MODEL ** -0.5),
        "w_kv": nrm((D_MODEL, 2 * D_ATTN), D_MODEL ** -0.5),
        "w_o": nrm((N_B_LAYERS, D_ATTN, D_MODEL), D_ATTN ** -0.5),
        "w_gate_dense": nrm((N_DENSE, D_MODEL, D_FF), D_MODEL ** -0.5),
        "w_up_dense": nrm((N_DENSE, D_MODEL, D_FF), D_MODEL ** -0.5),
        "w_down_dense": nrm((N_DENSE, D_FF, D_MODEL), D_FF ** -0.5),
        "w_router": nrm((N_MOE, D_MODEL, N_EXPERTS), D_MODEL ** -0.5),
        "w_gate_exp": nrm((N_MOE, N_EXPERTS, D_MODEL, D_EXPERT), D_MODEL ** -0.5),
        "w_up_exp": nrm((N_MOE, N_EXPERTS, D_MODEL, D_EXPERT), D_MODEL ** -0.5),
        "w_down_exp": nrm((N_MOE, N_EXPERTS, D_EXPERT, D_MODEL), D_EXPERT ** -0.5),
    }


def reference(x_prompt, x_sample, state_conv, cache_kv_g0, cache_kv_g1, cache_kv_g2,
              norm_mix, norm_ffn, norm_kv, norm_final,
              w_pw1, b_pw1, w_dw, b_dw, ln_conv_g, ln_conv_b, w_pw2, b_pw2,
              w_q, w_kv, w_o,
              w_gate_dense, w_up_dense, w_down_dense,
              w_router, w_gate_exp, w_up_exp, w_down_exp):
    caches = (cache_kv_g0, cache_kv_g1, cache_kv_g2)
    pos_p = jnp.arange(SEQ)
    pos_s = PAST_LEN + jnp.arange(DEC_SEQ)
    hp, hs = x_prompt, x_sample
    conv_p, conv_s = [], []
    k_p = v_p = k_s = v_s = None

    for layer in range(DEPTH):
        np_ = rms_norm(hp, norm_mix[layer])
        ns_ = rms_norm(hs, norm_mix[layer])
        if layer < N_A_LAYERS:
            a = layer
            prm = (w_pw1[a], b_pw1[a], w_dw[a], b_dw[a], ln_conv_g[a], ln_conv_b[a],
                   w_pw2[a], b_pw2[a])
            zero_state = jnp.zeros((BATCH, CONV_WIDTH - 1, D_CONV), hp.dtype)
            mp, st_p = conv_module(np_, zero_state, *prm)
            ms, st_s = conv_module(ns_, state_conv[a], *prm)
            conv_p.append(st_p)
            conv_s.append(st_s)
        else:
            b = layer - N_A_LAYERS
            q_p = project_q(np_, w_q[b], pos_p)
            q_s = project_q(ns_, w_q[b], pos_s)
            outs_p, lses_p, outs_s, lses_s = [], [], [], []
            for g, (window, dil) in enumerate(GROUPS):
                hsl = slice(g * HEADS_PER_GROUP, (g + 1) * HEADS_PER_GROUP)
                n_keys = window // dil
                o, l = dilated_band_attention(q_p[:, :, hsl], k_p[:, :, hsl], v_p[:, :, hsl],
                                              dil, n_keys)
                outs_p.append(o)
                lses_p.append(l)
                buf_len = caches[g].shape[1]
                k_all = jnp.concatenate([caches[g][:, :, 0].astype(k_s.dtype), k_s[:, :, hsl]], axis=1)
                v_all = jnp.concatenate([caches[g][:, :, 1].astype(v_s.dtype), v_s[:, :, hsl]], axis=1)
                o, l = dilated_cached_attention(q_s[:, :, hsl], k_all, v_all, dil, n_keys,
                                                buf_len, PAST_LEN)
                outs_s.append(o)
                lses_s.append(l)
            mp = combine_groups(outs_p, lses_p) @ w_o[b]
            ms = combine_groups(outs_s, lses_s) @ w_o[b]
        hp = hp + mp
        hs = hs + ms

        fp = rms_norm(hp, norm_ffn[layer])
        fs = rms_norm(hs, norm_ffn[layer])
        c = layer // 2
        if layer % 2 == 0:
            hp = hp + swiglu(fp, w_gate_dense[c], w_up_dense[c], w_down_dense[c])
            hs = hs + swiglu(fs, w_gate_dense[c], w_up_dense[c], w_down_dense[c])
        else:
            hp = hp + moe_swiglu(fp, w_router[c], w_gate_exp[c], w_up_exp[c], w_down_exp[c])
            hs = hs + moe_swiglu(fs, w_router[c], w_gate_exp[c], w_up_exp[c], w_down_exp[c])

        if layer == N_A_LAYERS - 1:
            k_p, v_p = project_shared_kv(hp, norm_kv, w_kv, pos_p)
            k_s, v_s = project_shared_kv(hs, norm_kv, w_kv, pos_s)

    y_prompt = rms_norm(hp, norm_final)
    y_sample = rms_norm(hs, norm_final)

    conv_prompt = jnp.stack(conv_p, axis=0)
    conv_sample = jnp.stack(conv_s, axis=0)
    kv_new_p, kv_new_s = [], []
    for g, (window, dil) in enumerate(GROUPS):
        hsl = slice(g * HEADS_PER_GROUP, (g + 1) * HEADS_PER_GROUP)
        keep = min(window, SEQ)
        kv_new_p.append(jnp.stack([k_p[:, -keep:, hsl], v_p[:, -keep:, hsl]], axis=2))
        kv_new_s.append(jnp.stack([k_s[:, :, hsl], v_s[:, :, hsl]], axis=2))
    return (y_prompt, y_sample, conv_prompt, conv_sample,
            kv_new_p[0], kv_new_s[0], kv_new_p[1], kv_new_s[1], kv_new_p[2], kv_new_s[2])
```

```python
import functools

import jax
import jax.numpy as jnp
from jax import lax
from jax.experimental import pallas as pl
from jax.experimental.pallas import tpu as pltpu

F32 = jnp.float32
BF16 = jnp.bfloat16

EPS = 1e-5
CONV_WIDTH = 31
HEAD_DIM = 64
ROT_DIM = HEAD_DIM // 4
HEADS_PER_GROUP = 4
GROUP_COLS = HEADS_PER_GROUP * HEAD_DIM
GROUPS = ((128, 1), (512, 4), (2048, 16))
N_GROUPS = len(GROUPS)
ROPE_THETA = 500000.0
TOP_K = 2

V7X_LANES = 128
V7X_SUBLANES = 8
V7X_VMEM_BYTES = 64 * 2**20
VMEM_LIMIT_BYTES = V7X_VMEM_BYTES - 8 * 2**20

PAST_LEN = 16384
HALO_ROWS = 32
TAP_ROWS = 32
CONV_ROW_CHUNK = 32
SAMPLE_Q_ROWS = 8


def _params(*sem):
    return pltpu.CompilerParams(dimension_semantics=sem, vmem_limit_bytes=VMEM_LIMIT_BYTES)


def _rms(x, w):
    ms = jnp.mean(x * x, axis=-1, keepdims=True)
    return (x * lax.rsqrt(ms + EPS)) * w


def _silu(x):
    return x * jax.nn.sigmoid(x)


def _dot(a, b):
    return jnp.dot(a, b, preferred_element_type=F32)


def _dot_nt(a, b):
    return lax.dot_general(a, b, (((1,), (1,)), ((), ())), preferred_element_type=F32)


def _row_tile(n_rows, want):
    t = min(n_rows, want)
    assert n_rows % t == 0 and (t % V7X_SUBLANES == 0 or t == n_rows)
    return t


def _const_spec(shape):
    return pl.BlockSpec(shape, lambda *_: (0,) * len(shape), pipeline_mode=pl.Buffered(1))


def _pw1_glu_kernel(x_ref, nw_ref, w_ref, b_ref, g_ref):
    d = x_ref.shape[1]
    xn = _rms(x_ref[...], nw_ref[...]).astype(BF16)
    u = _dot(xn, w_ref[...]) + b_ref[...]
    g_ref[...] = u[:, :d] * jax.nn.sigmoid(u[:, d:])


def pw1_glu(x, nw, w, b, tm):
    n, d = x.shape
    return pl.pallas_call(
        _pw1_glu_kernel,
        grid=(n // tm,),
        in_specs=[pl.BlockSpec((tm, d), lambda i: (i, 0)),
                  _const_spec((1, d)), _const_spec((d, 2 * d)), _const_spec((1, 2 * d))],
        out_specs=pl.BlockSpec((tm, d), lambda i: (i, 0)),
        out_shape=jax.ShapeDtypeStruct((n, d), F32),
        compiler_params=_params("parallel"),
        name="pw1_glu",
    )(x, nw, w, b)


def _conv_tail(c, x, lng, lnb, w2, b2):
    mu = jnp.mean(c, axis=-1, keepdims=True)
    cc = c - mu
    y = cc * lax.rsqrt(jnp.mean(cc * cc, axis=-1, keepdims=True) + EPS)
    y = _silu(y * lng + lnb)
    return x + _dot(y.astype(BF16), w2) + b2


def _conv_prompt_kernel(g_ref, halo_ref, x_ref, wdw_ref, bdw_ref, lng_ref, lnb_ref, w2_ref, b2_ref,
                        o_ref, win_ref, c_ref, *, tiles_per_seq):
    tm, d = g_ref.shape
    first = (pl.program_id(0) % tiles_per_seq) == 0
    win_ref[0:HALO_ROWS, :] = jnp.where(first, 0.0, halo_ref[...])
    win_ref[HALO_ROWS:HALO_ROWS + tm, :] = g_ref[...]
    lead = HALO_ROWS - (CONV_WIDTH - 1)

    def chunk(ci, carry):
        r0 = pl.multiple_of(ci * CONV_ROW_CHUNK, CONV_ROW_CHUNK)
        acc = jnp.broadcast_to(bdw_ref[...], (CONV_ROW_CHUNK, d))
        w = win_ref[pl.ds(r0, HALO_ROWS + CONV_ROW_CHUNK), :]
        for j in range(CONV_WIDTH):
            acc = acc + wdw_ref[j:j + 1, :] * w[lead + j:lead + j + CONV_ROW_CHUNK, :]
        c_ref[pl.ds(r0, CONV_ROW_CHUNK), :] = acc
        return carry

    lax.fori_loop(0, tm // CONV_ROW_CHUNK, chunk, 0)
    o_ref[...] = _conv_tail(c_ref[...], x_ref[...], lng_ref[...], lnb_ref[...], w2_ref[...], b2_ref[...])


def conv_prompt(g, x, seq_len, wdw, bdw, lng, lnb, w2, b2, tm):
    n, d = g.shape
    tiles_per_seq = seq_len // tm
    halo_per_tile = tm // HALO_ROWS
    return pl.pallas_call(
        functools.partial(_conv_prompt_kernel, tiles_per_seq=tiles_per_seq),
        grid=(n // tm,),
        in_specs=[pl.BlockSpec((tm, d), lambda i: (i, 0)),
                  pl.BlockSpec((HALO_ROWS, d), lambda i: (jnp.maximum(i * halo_per_tile - 1, 0), 0)),
                  pl.BlockSpec((tm, d), lambda i: (i, 0)),
                  _const_spec((TAP_ROWS, d)), _const_spec((1, d)), _const_spec((1, d)), _const_spec((1, d)),
                  _const_spec((d, d)), _const_spec((1, d))],
        out_specs=pl.BlockSpec((tm, d), lambda i: (i, 0)),
        out_shape=jax.ShapeDtypeStruct((n, d), F32),
        scratch_shapes=[pltpu.VMEM((HALO_ROWS + tm, d), F32), pltpu.VMEM((tm, d), F32)],
        compiler_params=_params("parallel"),
        name="conv_prompt",
    )(g, g, x, wdw, bdw, lng, lnb, w2, b2)


def _conv_sample_kernel(full_ref, x_ref, wdw_ref, bdw_ref, lng_ref, lnb_ref, w2_ref, b2_ref, o_ref, *, n_t):
    nb, d = full_ref.shape[1], full_ref.shape[2]
    for t in range(n_t):
        acc = jnp.broadcast_to(bdw_ref[...], (nb, d))
        for j in range(CONV_WIDTH):
            acc = acc + wdw_ref[j:j + 1, :] * full_ref[t + j]
        rows = slice(t * nb, (t + 1) * nb)
        o_ref[rows, :] = _conv_tail(acc, x_ref[rows, :], lng_ref[...], lnb_ref[...], w2_ref[...], b2_ref[...])


def conv_sample(full_tm, x, wdw, bdw, lng, lnb, w2, b2):
    rows_full, nb, d = full_tm.shape
    n_t = rows_full - (CONV_WIDTH - 1)
    n = n_t * nb
    return pl.pallas_call(
        functools.partial(_conv_sample_kernel, n_t=n_t),
        grid=(1,),
        in_specs=[_const_spec((rows_full, nb, d)), _const_spec((n, d)),
                  _const_spec((TAP_ROWS, d)), _const_spec((1, d)), _const_spec((1, d)), _const_spec((1, d)),
                  _const_spec((d, d)), _const_spec((1, d))],
        out_specs=pl.BlockSpec((n, d), lambda i: (0, 0)),
        out_shape=jax.ShapeDtypeStruct((n, d), F32),
        compiler_params=_params("arbitrary"),
        name="conv_sample",
    )(full_tm, x, wdw, bdw, lng, lnb, w2, b2)


def _ffn_kernel(x_ref, nw_ref, wg_ref, wu_ref, wd_ref, o_ref, h_ref, *, ff_chunk):
    ff = wg_ref.shape[1]
    x = x_ref[...]
    xn = _rms(x, nw_ref[...]).astype(BF16)
    for c in range(ff // ff_chunk):
        cols = slice(c * ff_chunk, (c + 1) * ff_chunk)
        h = _silu(_dot(xn, wg_ref[:, cols])) * _dot(xn, wu_ref[:, cols])
        h_ref[:, cols] = h.astype(BF16)
    o_ref[...] = x + _dot(h_ref[...], wd_ref[...])


def _ff_chunk(ff):
    for c in (512, 256, 128):
        if ff % c == 0:
            return c
    return ff


def ffn(x, nw, wg, wu, wd, tm):
    n, d = x.shape
    ff = wg.shape[1]
    return pl.pallas_call(
        functools.partial(_ffn_kernel, ff_chunk=_ff_chunk(ff)),
        grid=(n // tm,),
        in_specs=[pl.BlockSpec((tm, d), lambda i: (i, 0)), _const_spec((1, d)),
                  _const_spec((d, ff)), _const_spec((d, ff)), _const_spec((ff, d))],
        out_specs=pl.BlockSpec((tm, d), lambda i: (i, 0)),
        out_shape=jax.ShapeDtypeStruct((n, d), F32),
        scratch_shapes=[pltpu.VMEM((tm, ff), BF16)],
        compiler_params=_params("parallel"),
        name="ffn_dense",
    )(x, nw, wg, wu, wd)


def _proj_rope_kernel(x_ref, nw_ref, w_ref, cos_ref, sin_ref, o_ref, *, rope_cols):
    n_out = w_ref.shape[1]
    y = _dot(_rms(x_ref[...], nw_ref[...]).astype(BF16), w_ref[...])
    cos_t, sin_t = cos_ref[...], sin_ref[...]
    lane = lax.broadcasted_iota(jnp.int32, cos_t.shape, 1)
    low_half = (lane % HEAD_DIM) < (ROT_DIM // 2)
    for cb in range(n_out // V7X_LANES):
        cols = slice(cb * V7X_LANES, (cb + 1) * V7X_LANES)
        yb = y[:, cols]
        if cb * V7X_LANES < rope_cols:
            partner = jnp.where(low_half,
                                pltpu.roll(yb, V7X_LANES - ROT_DIM // 2, axis=1),
                                pltpu.roll(yb, ROT_DIM // 2, axis=1))
            yb = yb * cos_t + partner * sin_t
        o_ref[:, cols] = yb.astype(o_ref.dtype)


def proj_rope(x, nw, w, cos_t, sin_t, rope_cols, out_dtype, tm):
    n, d = x.shape
    n_out = w.shape[1]
    table_tiles = cos_t.shape[0] // tm
    return pl.pallas_call(
        functools.partial(_proj_rope_kernel, rope_cols=rope_cols),
        grid=(n // tm,),
        in_specs=[pl.BlockSpec((tm, d), lambda i: (i, 0)), _const_spec((1, d)), _const_spec((d, n_out)),
                  pl.BlockSpec((tm, V7X_LANES), lambda i: (i % table_tiles, 0)),
                  pl.BlockSpec((tm, V7X_LANES), lambda i: (i % table_tiles, 0))],
        out_specs=pl.BlockSpec((tm, n_out), lambda i: (i, 0)),
        out_shape=jax.ShapeDtypeStruct((n, n_out), out_dtype),
        compiler_params=_params("parallel"),
        name="proj_rope",
    )(x, nw, w, cos_t, sin_t)


def rope_tables(pos):
    half = ROT_DIM // 2
    inv_freq = ROPE_THETA ** (-jnp.arange(half, dtype=F32) / half)
    ang = pos.astype(F32)[:, None] * inv_freq[None, :]
    cos, sin = jnp.cos(ang), jnp.sin(ang)
    ones, zeros = jnp.ones_like(cos), jnp.zeros_like(sin)
    pad = (HEAD_DIM - ROT_DIM) // half
    cos_h = jnp.concatenate([cos, cos] + [ones] * pad, axis=1)
    sin_h = jnp.concatenate([-sin, sin] + [zeros] * pad, axis=1)
    reps = V7X_LANES // HEAD_DIM
    return jnp.tile(cos_h, (1, reps)), jnp.tile(sin_h, (1, reps))


def _head_masks(shape):
    lane = lax.broadcasted_iota(jnp.int32, shape, 1)
    return [(lane // HEAD_DIM) == h for h in range(HEADS_PER_GROUP)]


def _attn_prompt_kernel(q_ref, kc_ref, kp_ref, vc_ref, vp_ref, o_ref, l_ref, *, n_keys):
    i = pl.program_id(2)
    tq = q_ref.shape[0]
    q = q_ref[...]
    k = jnp.concatenate([kp_ref[...], kc_ref[...]], axis=0).astype(BF16)
    v = jnp.concatenate([vp_ref[...], vc_ref[...]], axis=0).astype(BF16)
    hm_q = _head_masks(q.shape)
    hm_v = _head_masks(v.shape)
    qs = jnp.concatenate([jnp.where(m, q, jnp.zeros_like(q)) for m in hm_q], axis=0)
    s = _dot_nt(qs, k) * (HEAD_DIM ** -0.5)
    qi = lax.broadcasted_iota(jnp.int32, s.shape, 0) % tq
    kj = lax.broadcasted_iota(jnp.int32, s.shape, 1)
    rel = qi + tq - kj
    ok = (rel >= 0) & (rel <= n_keys) & ((kj >= tq) | (i > 0))
    s = jnp.where(ok, s, -jnp.inf)
    m = jnp.max(s, axis=-1, keepdims=True)
    p = jnp.exp(s - m)
    l = jnp.sum(p, axis=-1, keepdims=True)
    pn = (p / l).astype(BF16)
    lse = m + jnp.log(l)
    o = jnp.zeros(o_ref.shape, F32)
    lmap = jnp.zeros(l_ref.shape, F32)
    for h in range(HEADS_PER_GROUP):
        rows = slice(h * tq, (h + 1) * tq)
        o = o + _dot(pn[rows], jnp.where(hm_v[h], v, jnp.zeros_like(v)))
        lmap = jnp.where(hm_q[h], lse[rows], lmap)
    o_ref[...] = o
    l_ref[...] = lmap


def attn_prompt(q, kv, group, batch, seq_len):
    window, dil = GROUPS[group]
    n_keys = window // dil
    tq = n_keys
    L = seq_len // dil
    nblk = L // tq
    n_attn = N_GROUPS * GROUP_COLS
    qv = q.reshape(batch, L, dil * n_attn)
    kvv = kv.reshape(batch, L, dil * 2 * n_attn)
    blk = (None, tq, GROUP_COLS)
    qcol = lambda b, r, i: (b, i, r * N_GROUPS + group)
    kcol = lambda b, r, i: (b, i, r * 2 * N_GROUPS + group)
    kcol_prev = lambda b, r, i: (b, jnp.maximum(i - 1, 0), r * 2 * N_GROUPS + group)
    vcol = lambda b, r, i: (b, i, r * 2 * N_GROUPS + N_GROUPS + group)
    vcol_prev = lambda b, r, i: (b, jnp.maximum(i - 1, 0), r * 2 * N_GROUPS + N_GROUPS + group)
    ocol = lambda b, r, i: (b, i, r)
    o, lmap = pl.pallas_call(
        functools.partial(_attn_prompt_kernel, n_keys=n_keys),
        grid=(batch, dil, nblk),
        in_specs=[pl.BlockSpec(blk, qcol), pl.BlockSpec(blk, kcol), pl.BlockSpec(blk, kcol_prev),
                  pl.BlockSpec(blk, vcol), pl.BlockSpec(blk, vcol_prev)],
        out_specs=[pl.BlockSpec(blk, ocol), pl.BlockSpec(blk, ocol)],
        out_shape=[jax.ShapeDtypeStruct((batch, L, dil * GROUP_COLS), F32)] * 2,
        compiler_params=_params("parallel", "parallel", "parallel"),
        name=f"attn_prompt_g{group}",
    )(qv, kvv, kvv, kvv, kvv)
    return o.reshape(batch * seq_len, GROUP_COLS), lmap.reshape(batch * seq_len, GROUP_COLS)


def _attn_sample_kernel(q_ref, kvn_ref, c0_ref, c1_ref, c2_ref, o_ref, l_ref, k_all, v_all, *, n_new):
    nq = q_ref.shape[0]
    n_attn = N_GROUPS * GROUP_COLS
    for g, c_ref in enumerate((c0_ref, c1_ref, c2_ref)):
        window, dil = GROUPS[g]
        buf = c_ref.shape[0]
        n_all = buf + V7X_LANES
        gc = slice(g * GROUP_COLS, (g + 1) * GROUP_COLS)
        k_all[0:buf, :] = c_ref[:, 0:GROUP_COLS]
        v_all[0:buf, :] = c_ref[:, GROUP_COLS:2 * GROUP_COLS]
        k_all[buf:n_all, :] = jnp.zeros((V7X_LANES, GROUP_COLS), F32)
        v_all[buf:n_all, :] = jnp.zeros((V7X_LANES, GROUP_COLS), F32)
        k_all[buf:buf + n_new, :] = kvn_ref[:, gc]
        v_all[buf:buf + n_new, :] = kvn_ref[:, n_attn + g * GROUP_COLS:n_attn + (g + 1) * GROUP_COLS]
        k = k_all[0:n_all, :].astype(BF16)
        v = v_all[0:n_all, :].astype(BF16)
        q = q_ref[:, gc]
        t = lax.broadcasted_iota(jnp.int32, (nq, n_all), 0)
        key = lax.broadcasted_iota(jnp.int32, (nq, n_all), 1)
        rel = buf + t - key
        ok = (rel >= 0) & ((rel & (dil - 1)) == 0) & (rel <= window) & (key < buf + n_new)
        hm_q = _head_masks(q.shape)
        hm_v = _head_masks(v.shape)
        o = jnp.zeros((nq, GROUP_COLS), F32)
        lmap = jnp.zeros((nq, GROUP_COLS), F32)
        for h in range(HEADS_PER_GROUP):
            s = _dot_nt(jnp.where(hm_q[h], q, jnp.zeros_like(q)), k) * (HEAD_DIM ** -0.5)
            s = jnp.where(ok, s, -jnp.inf)
            m = jnp.max(s, axis=-1, keepdims=True)
            p = jnp.exp(s - m)
            l = jnp.sum(p, axis=-1, keepdims=True)
            o = o + _dot((p / l).astype(BF16), jnp.where(hm_v[h], v, jnp.zeros_like(v)))
            lmap = jnp.where(hm_q[h], m + jnp.log(l), lmap)
        o_ref[:, gc] = o
        l_ref[:, gc] = lmap


def attn_sample(q, kv_new, caches, n_new):
    nb, nq, n_attn = q.shape
    max_buf = max(c.shape[1] for c in caches)
    in_specs = [pl.BlockSpec((None, nq, n_attn), lambda b: (b, 0, 0)),
                pl.BlockSpec((None, n_new, 2 * n_attn), lambda b: (b, 0, 0))]
    in_specs += [pl.BlockSpec((None, c.shape[1], 2 * GROUP_COLS), lambda b: (b, 0, 0)) for c in caches]
    return pl.pallas_call(
        functools.partial(_attn_sample_kernel, n_new=n_new),
        grid=(nb,),
        in_specs=in_specs,
        out_specs=[pl.BlockSpec((None, nq, n_attn), lambda b: (b, 0, 0))] * 2,
        out_shape=[jax.ShapeDtypeStruct((nb, nq, n_attn), F32)] * 2,
        scratch_shapes=[pltpu.VMEM((max_buf + V7X_LANES, GROUP_COLS), F32)] * 2,
        compiler_params=_params("parallel"),
        name="attn_sample",
    )(q, kv_new, *caches)


def _combine_wo_kernel(o0_ref, o1_ref, o2_ref, l0_ref, l1_ref, l2_ref, x_ref, w_ref, out_ref):
    ls = [l0_ref[...], l1_ref[...], l2_ref[...]]
    mx = jnp.maximum(jnp.maximum(ls[0], ls[1]), ls[2])
    es = [jnp.exp(l - mx) for l in ls]
    tot = es[0] + es[1] + es[2]
    os_ = [o0_ref[...], o1_ref[...], o2_ref[...]]
    acc = x_ref[...]
    for g in range(N_GROUPS):
        og = (os_[g] * (es[g] / tot)).astype(BF16)
        acc = acc + _dot(og, w_ref[g * GROUP_COLS:(g + 1) * GROUP_COLS, :])
    out_ref[...] = acc


def combine_wo(os_, ls, x, w_o, tm):
    n, d = x.shape
    gspec = pl.BlockSpec((tm, GROUP_COLS), lambda i: (i, 0))
    return pl.pallas_call(
        _combine_wo_kernel,
        grid=(n // tm,),
        in_specs=[gspec] * 6 + [pl.BlockSpec((tm, d), lambda i: (i, 0)), _const_spec(w_o.shape)],
        out_specs=pl.BlockSpec((tm, d), lambda i: (i, 0)),
        out_shape=jax.ShapeDtypeStruct((n, d), F32),
        compiler_params=_params("parallel"),
        name="combine_wo",
    )(*os_, *ls, x, w_o)


def _router_kernel(x_ref, nw_ref, wr_ref, gate_ref, *, n_experts):
    xn = _rms(x_ref[...], nw_ref[...])
    xh = xn.astype(BF16)
    xl = (xn - xh.astype(F32)).astype(BF16)
    wr = wr_ref[...]
    wh = wr.astype(BF16)
    wl = (wr - wh.astype(F32)).astype(BF16)
    logits = _dot(xh, wh) + (_dot(xl, wh) + _dot(xh, wl))
    lane = lax.broadcasted_iota(jnp.int32, logits.shape, 1)
    logits = jnp.where(lane < n_experts, logits, -jnp.inf)
    v1 = jnp.max(logits, axis=-1, keepdims=True)
    i1 = jnp.min(jnp.where(logits == v1, lane, V7X_LANES), axis=-1, keepdims=True)
    rest = jnp.where(lane == i1, -jnp.inf, logits)
    v2 = jnp.max(rest, axis=-1, keepdims=True)
    i2 = jnp.min(jnp.where(rest == v2, lane, V7X_LANES), axis=-1, keepdims=True)
    e2 = jnp.exp(v2 - v1)
    den = 1.0 + e2
    dense = jnp.where(lane == i1, 1.0 / den, jnp.where(lane == i2, e2 / den, 0.0))
    gate_ref[...] = dense[:, :n_experts]


def router(x, nw, w_router, tm):
    n, d = x.shape
    n_experts = w_router.shape[1]
    wr = jnp.pad(w_router, ((0, 0), (0, V7X_LANES - n_experts)))
    return pl.pallas_call(
        functools.partial(_router_kernel, n_experts=n_experts),
        grid=(n // tm,),
        in_specs=[pl.BlockSpec((tm, d), lambda i: (i, 0)), _const_spec((1, d)), _const_spec((d, V7X_LANES))],
        out_specs=pl.BlockSpec((tm, n_experts), lambda i: (i, 0)),
        out_shape=jax.ShapeDtypeStruct((n, n_experts), F32),
        compiler_params=_params("parallel"),
        name="router",
    )(x, nw, wr)


def _moe_kernel(x_ref, gate_ref, nw_ref, nf_ref, wg_ref, wu_ref, wd_ref, y_ref, xn_ref, acc_e, acc_t):
    e, c = pl.program_id(1), pl.program_id(2)
    last_e, last_c = pl.num_programs(1) - 1, pl.num_programs(2) - 1

    @pl.when((e == 0) & (c == 0))
    def _():
        xn_ref[...] = _rms(x_ref[...], nw_ref[...]).astype(BF16)
        acc_t[...] = jnp.zeros_like(acc_t)

    @pl.when(c == 0)
    def _():
        acc_e[...] = jnp.zeros_like(acc_e)

    xn = xn_ref[...]
    h = _silu(_dot(xn, wg_ref[...])) * _dot(xn, wu_ref[...])
    acc_e[...] += _dot(h.astype(BF16), wd_ref[...])

    @pl.when(c == last_c)
    def _():
        gates = gate_ref[...]
        lane = lax.broadcasted_iota(jnp.int32, gates.shape, 1)
        ge = jnp.sum(jnp.where(lane == e, gates, 0.0), axis=-1, keepdims=True)
        acc_t[...] += ge * acc_e[...]

    @pl.when((e == last_e) & (c == last_c))
    def _():
        y_ref[...] = _rms(x_ref[...] + acc_t[...], nf_ref[...])


def moe_final(x, gates, nw, nf, wg, wu, wd, tm):
    n, d = x.shape
    n_experts, _, de = wg.shape
    tf = _ff_chunk(de)
    return pl.pallas_call(
        _moe_kernel,
        grid=(n // tm, n_experts, de // tf),
        in_specs=[pl.BlockSpec((tm, d), lambda i, e, c: (i, 0)),
                  pl.BlockSpec((tm, n_experts), lambda i, e, c: (i, 0)),
                  pl.BlockSpec((1, d), lambda i, e, c: (0, 0)), pl.BlockSpec((1, d), lambda i, e, c: (0, 0)),
                  pl.BlockSpec((None, d, tf), lambda i, e, c: (e, 0, c)),
                  pl.BlockSpec((None, d, tf), lambda i, e, c: (e, 0, c)),
                  pl.BlockSpec((None, tf, d), lambda i, e, c: (e, c, 0))],
        out_specs=pl.BlockSpec((tm, d), lambda i, e, c: (i, 0)),
        out_shape=jax.ShapeDtypeStruct((n, d), F32),
        scratch_shapes=[pltpu.VMEM((tm, d), BF16), pltpu.VMEM((tm, d), F32), pltpu.VMEM((tm, d), F32)],
        compiler_params=_params("parallel", "arbitrary", "arbitrary"),
        name="moe_final",
    )(x, gates, nw, nf, wg, wu, wd)


def kernel(x_prompt, x_sample, state_conv, cache_kv_g0, cache_kv_g1, cache_kv_g2, norm_mix, norm_ffn, norm_kv, norm_final, w_pw1, b_pw1, w_dw, b_dw, ln_conv_g, ln_conv_b, w_pw2, b_pw2, w_q, w_kv, w_o, w_gate_dense, w_up_dense, w_down_dense, w_router, w_gate_exp, w_up_exp, w_down_exp):
    batch, seq_len, d = x_prompt.shape
    dec_batch, dec_seq, _ = x_sample.shape
    caches = (cache_kv_g0, cache_kv_g1, cache_kv_g2)
    n_attn = N_GROUPS * GROUP_COLS
    assert norm_mix.shape[0] == 2 and state_conv.shape[0] == 1 and w_q.shape[0] == 1
    assert all(c.shape[1] == w for c, (w, _) in zip(caches, GROUPS))
    assert dec_seq <= SAMPLE_Q_ROWS

    row = lambda v: v.reshape(1, -1).astype(F32)
    bf = lambda w: w.astype(BF16)
    n_p = batch * seq_len
    n_s = dec_batch * dec_seq
    tm_p = _row_tile(seq_len, 512)
    tm_s = n_s

    wdw = jnp.pad(w_dw[0], ((0, TAP_ROWS - CONV_WIDTH), (0, 0)))
    conv_w = (wdw, row(b_dw[0]), row(ln_conv_g[0]), row(ln_conv_b[0]), bf(w_pw2[0]), row(b_pw2[0]))
    w_pw1_b, b_pw1_r = bf(w_pw1[0]), row(b_pw1[0])
    ffn_w = (bf(w_gate_dense[0]), bf(w_up_dense[0]), bf(w_down_dense[0]))
    w_kv_b, w_q_b, w_o_b = bf(w_kv), bf(w_q[0]), bf(w_o[0])
    moe_w = (bf(w_gate_exp[0]), bf(w_up_exp[0]), bf(w_down_exp[0]))

    cos_p, sin_p = rope_tables(jnp.arange(seq_len))
    cos_s, sin_s = rope_tables(PAST_LEN + jnp.arange(n_s) // dec_batch)

    hp = x_prompt.reshape(n_p, d)
    hs = x_sample.transpose(1, 0, 2).reshape(n_s, d)
    g_p = pw1_glu(hp, row(norm_mix[0]), w_pw1_b, b_pw1_r, tm_p)
    g_s = pw1_glu(hs, row(norm_mix[0]), w_pw1_b, b_pw1_r, tm_s)
    hp = conv_prompt(g_p, hp, seq_len, *conv_w, tm_p)
    full_s = jnp.concatenate([state_conv[0].transpose(1, 0, 2), g_s.reshape(dec_seq, dec_batch, d)], axis=0)
    hs = conv_sample(full_s, hs, *conv_w)
    conv_prompt_out = g_p.reshape(batch, seq_len, d)[:, seq_len - (CONV_WIDTH - 1):][None]
    conv_sample_out = full_s[dec_seq:].transpose(1, 0, 2)[None]

    hp = ffn(hp, row(norm_ffn[0]), *ffn_w, tm_p)
    hs = ffn(hs, row(norm_ffn[0]), *ffn_w, tm_s)

    kv_p = proj_rope(hp, row(norm_kv), w_kv_b, cos_p, sin_p, n_attn, F32, tm_p)
    kv_s = proj_rope(hs, row(norm_kv), w_kv_b, cos_s, sin_s, n_attn, F32, tm_s)
    q_p = proj_rope(hp, row(norm_mix[1]), w_q_b, cos_p, sin_p, n_attn, BF16, tm_p)
    q_s = proj_rope(hs, row(norm_mix[1]), w_q_b, cos_s, sin_s, n_attn, BF16, tm_s)

    att_p = [attn_prompt(q_p, kv_p, g, batch, seq_len) for g in range(N_GROUPS)]
    hp = combine_wo([a[0] for a in att_p], [a[1] for a in att_p], hp, w_o_b, tm_p)

    to_bm = lambda a: a.reshape(dec_seq, dec_batch, -1).transpose(1, 0, 2)
    hs = to_bm(hs).reshape(n_s, d)
    kv_s = to_bm(kv_s)
    q_s = jnp.pad(to_bm(q_s), ((0, 0), (0, SAMPLE_Q_ROWS - dec_seq), (0, 0)))
    caches2 = [c.reshape(dec_batch, c.shape[1], 2 * GROUP_COLS) for c in caches]
    o_s, l_s = attn_sample(q_s, kv_s, caches2, dec_seq)
    grp = lambda a, g: a[:, :dec_seq, g * GROUP_COLS:(g + 1) * GROUP_COLS].reshape(n_s, GROUP_COLS)
    hs = combine_wo([grp(o_s, g) for g in range(N_GROUPS)], [grp(l_s, g) for g in range(N_GROUPS)], hs, w_o_b, tm_s)

    gates_p = router(hp, row(norm_ffn[1]), w_router[0], tm_p)
    gates_s = router(hs, row(norm_ffn[1]), w_router[0], tm_s)
    y_p = moe_final(hp, gates_p, row(norm_ffn[1]), row(norm_final), *moe_w, _row_tile(seq_len, 1024))
    y_s = moe_final(hs, gates_s, row(norm_ffn[1]), row(norm_final), *moe_w, tm_s)

    kv_p5 = kv_p.reshape(batch, seq_len, 2, N_GROUPS * HEADS_PER_GROUP, HEAD_DIM)
    kv_s5 = kv_s.reshape(dec_batch, dec_seq, 2, N_GROUPS * HEADS_PER_GROUP, HEAD_DIM)
    kv_out = []
    for g, (window, _) in enumerate(GROUPS):
        keep = min(window, seq_len)
        heads = slice(g * HEADS_PER_GROUP, (g + 1) * HEADS_PER_GROUP)
        kv_out += [kv_p5[:, seq_len - keep:, :, heads], kv_s5[:, :, :, heads]]
    return (y_p.reshape(batch, seq_len, d), y_s.reshape(dec_batch, dec_seq, d),
            conv_prompt_out, conv_sample_out, *kv_out)
```

```python
import functools

import jax
import jax.numpy as jnp
from jax import lax
from jax.experimental import pallas as pl
from jax.experimental.pallas import tpu as pltpu

F32 = jnp.float32
BF16 = jnp.bfloat16

EPS = 1e-5
CONV_WIDTH = 31
HEAD_DIM = 64
ROT_DIM = HEAD_DIM // 4
HEADS_PER_GROUP = 4
GROUP_COLS = HEADS_PER_GROUP * HEAD_DIM
GROUPS = ((128, 1), (512, 4), (2048, 16))
N_GROUPS = len(GROUPS)
ROPE_THETA = 500000.0
TOP_K = 2

V7X_LANES = 128
V7X_SUBLANES = 8
V7X_VMEM_BYTES = 64 * 2**20
VMEM_LIMIT_BYTES = V7X_VMEM_BYTES - 8 * 2**20

PAST_LEN = 16384
HALO_ROWS = 32
TAP_ROWS = 32
CONV_ROW_CHUNK = 32
SAMPLE_Q_ROWS = 8
BF16_SUBLANES = 2 * V7X_SUBLANES
META_COLS = 8
SEG_ALIGN = V7X_SUBLANES
MOE_ROW_TILE = 1024


def _params(*sem):
    return pltpu.CompilerParams(dimension_semantics=sem, vmem_limit_bytes=VMEM_LIMIT_BYTES)


def _rms(x, w):
    ms = jnp.mean(x * x, axis=-1, keepdims=True)
    return (x * lax.rsqrt(ms + EPS)) * w


def _silu(x):
    return x * jax.nn.sigmoid(x)


def _dot(a, b):
    return jnp.dot(a, b, preferred_element_type=F32)


def _dot_nt(a, b):
    return lax.dot_general(a, b, (((1,), (1,)), ((), ())), preferred_element_type=F32)


def _row_tile(n_rows, want):
    t = min(n_rows, want)
    assert n_rows % t == 0 and (t % V7X_SUBLANES == 0 or t == n_rows)
    return t


def _const_spec(shape):
    return pl.BlockSpec(shape, lambda *_: (0,) * len(shape), pipeline_mode=pl.Buffered(1))


def _pw1_glu_kernel(x_ref, nw_ref, w_ref, b_ref, g_ref):
    d = x_ref.shape[1]
    xn = _rms(x_ref[...], nw_ref[...]).astype(BF16)
    u = _dot(xn, w_ref[...]) + b_ref[...]
    g_ref[...] = u[:, :d] * jax.nn.sigmoid(u[:, d:])


def pw1_glu(x, nw, w, b, tm):
    n, d = x.shape
    return pl.pallas_call(
        _pw1_glu_kernel,
        grid=(n // tm,),
        in_specs=[pl.BlockSpec((tm, d), lambda i: (i, 0)),
                  _const_spec((1, d)), _const_spec((d, 2 * d)), _const_spec((1, 2 * d))],
        out_specs=pl.BlockSpec((tm, d), lambda i: (i, 0)),
        out_shape=jax.ShapeDtypeStruct((n, d), F32),
        compiler_params=_params("parallel"),
        name="pw1_glu",
    )(x, nw, w, b)


def _conv_tail(c, x, lng, lnb, w2, b2):
    mu = jnp.mean(c, axis=-1, keepdims=True)
    cc = c - mu
    y = cc * lax.rsqrt(jnp.mean(cc * cc, axis=-1, keepdims=True) + EPS)
    y = _silu(y * lng + lnb)
    return x + _dot(y.astype(BF16), w2) + b2


def _conv_prompt_kernel(g_ref, halo_ref, x_ref, wdw_ref, bdw_ref, lng_ref, lnb_ref, w2_ref, b2_ref,
                        o_ref, win_ref, c_ref, *, tiles_per_seq):
    tm, d = g_ref.shape
    first = (pl.program_id(0) % tiles_per_seq) == 0
    win_ref[0:HALO_ROWS, :] = jnp.where(first, 0.0, halo_ref[...])
    win_ref[HALO_ROWS:HALO_ROWS + tm, :] = g_ref[...]
    lead = HALO_ROWS - (CONV_WIDTH - 1)

    def chunk(ci, carry):
        r0 = pl.multiple_of(ci * CONV_ROW_CHUNK, CONV_ROW_CHUNK)
        acc = jnp.broadcast_to(bdw_ref[...], (CONV_ROW_CHUNK, d))
        w = win_ref[pl.ds(r0, HALO_ROWS + CONV_ROW_CHUNK), :]
        for j in range(CONV_WIDTH):
            acc = acc + wdw_ref[j:j + 1, :] * w[lead + j:lead + j + CONV_ROW_CHUNK, :]
        c_ref[pl.ds(r0, CONV_ROW_CHUNK), :] = acc
        return carry

    lax.fori_loop(0, tm // CONV_ROW_CHUNK, chunk, 0)
    o_ref[...] = _conv_tail(c_ref[...], x_ref[...], lng_ref[...], lnb_ref[...], w2_ref[...], b2_ref[...])


def conv_prompt(g, x, seq_len, wdw, bdw, lng, lnb, w2, b2, tm):
    n, d = g.shape
    tiles_per_seq = seq_len // tm
    halo_per_tile = tm // HALO_ROWS
    return pl.pallas_call(
        functools.partial(_conv_prompt_kernel, tiles_per_seq=tiles_per_seq),
        grid=(n // tm,),
        in_specs=[pl.BlockSpec((tm, d), lambda i: (i, 0)),
                  pl.BlockSpec((HALO_ROWS, d), lambda i: (jnp.maximum(i * halo_per_tile - 1, 0), 0)),
                  pl.BlockSpec((tm, d), lambda i: (i, 0)),
                  _const_spec((TAP_ROWS, d)), _const_spec((1, d)), _const_spec((1, d)), _const_spec((1, d)),
                  _const_spec((d, d)), _const_spec((1, d))],
        out_specs=pl.BlockSpec((tm, d), lambda i: (i, 0)),
        out_shape=jax.ShapeDtypeStruct((n, d), F32),
        scratch_shapes=[pltpu.VMEM((HALO_ROWS + tm, d), F32), pltpu.VMEM((tm, d), F32)],
        compiler_params=_params("parallel"),
        name="conv_prompt",
    )(g, g, x, wdw, bdw, lng, lnb, w2, b2)


def _conv_sample_kernel(full_ref, x_ref, wdw_ref, bdw_ref, lng_ref, lnb_ref, w2_ref, b2_ref, o_ref, *, n_t):
    nb, d = full_ref.shape[1], full_ref.shape[2]
    for t in range(n_t):
        acc = jnp.broadcast_to(bdw_ref[...], (nb, d))
        for j in range(CONV_WIDTH):
            acc = acc + wdw_ref[j:j + 1, :] * full_ref[t + j]
        rows = slice(t * nb, (t + 1) * nb)
        o_ref[rows, :] = _conv_tail(acc, x_ref[rows, :], lng_ref[...], lnb_ref[...], w2_ref[...], b2_ref[...])


def conv_sample(full_tm, x, wdw, bdw, lng, lnb, w2, b2):
    rows_full, nb, d = full_tm.shape
    n_t = rows_full - (CONV_WIDTH - 1)
    n = n_t * nb
    return pl.pallas_call(
        functools.partial(_conv_sample_kernel, n_t=n_t),
        grid=(1,),
        in_specs=[_const_spec((rows_full, nb, d)), _const_spec((n, d)),
                  _const_spec((TAP_ROWS, d)), _const_spec((1, d)), _const_spec((1, d)), _const_spec((1, d)),
                  _const_spec((d, d)), _const_spec((1, d))],
        out_specs=pl.BlockSpec((n, d), lambda i: (0, 0)),
        out_shape=jax.ShapeDtypeStruct((n, d), F32),
        compiler_params=_params("arbitrary"),
        name="conv_sample",
    )(full_tm, x, wdw, bdw, lng, lnb, w2, b2)


def _ffn_kernel(x_ref, nw_ref, wg_ref, wu_ref, wd_ref, o_ref, h_ref, *, ff_chunk):
    ff = wg_ref.shape[1]
    x = x_ref[...]
    xn = _rms(x, nw_ref[...]).astype(BF16)
    for c in range(ff // ff_chunk):
        cols = slice(c * ff_chunk, (c + 1) * ff_chunk)
        h = _silu(_dot(xn, wg_ref[:, cols])) * _dot(xn, wu_ref[:, cols])
        h_ref[:, cols] = h.astype(BF16)
    o_ref[...] = x + _dot(h_ref[...], wd_ref[...])


def _ff_chunk(ff):
    for c in (512, 256, 128):
        if ff % c == 0:
            return c
    return ff


def ffn(x, nw, wg, wu, wd, tm):
    n, d = x.shape
    ff = wg.shape[1]
    return pl.pallas_call(
        functools.partial(_ffn_kernel, ff_chunk=_ff_chunk(ff)),
        grid=(n // tm,),
        in_specs=[pl.BlockSpec((tm, d), lambda i: (i, 0)), _const_spec((1, d)),
                  _const_spec((d, ff)), _const_spec((d, ff)), _const_spec((ff, d))],
        out_specs=pl.BlockSpec((tm, d), lambda i: (i, 0)),
        out_shape=jax.ShapeDtypeStruct((n, d), F32),
        scratch_shapes=[pltpu.VMEM((tm, ff), BF16)],
        compiler_params=_params("parallel"),
        name="ffn_dense",
    )(x, nw, wg, wu, wd)


def _proj_rope_kernel(x_ref, nw_ref, w_ref, cos_ref, sin_ref, o_ref, *, rope_cols):
    n_out = w_ref.shape[1]
    y = _dot(_rms(x_ref[...], nw_ref[...]).astype(BF16), w_ref[...])
    cos_t, sin_t = cos_ref[...], sin_ref[...]
    lane = lax.broadcasted_iota(jnp.int32, cos_t.shape, 1)
    low_half = (lane % HEAD_DIM) < (ROT_DIM // 2)
    for cb in range(n_out // V7X_LANES):
        cols = slice(cb * V7X_LANES, (cb + 1) * V7X_LANES)
        yb = y[:, cols]
        if cb * V7X_LANES < rope_cols:
            partner = jnp.where(low_half,
                                pltpu.roll(yb, V7X_LANES - ROT_DIM // 2, axis=1),
                                pltpu.roll(yb, ROT_DIM // 2, axis=1))
            yb = yb * cos_t + partner * sin_t
        o_ref[:, cols] = yb.astype(o_ref.dtype)


def proj_rope(x, nw, w, cos_t, sin_t, rope_cols, out_dtype, tm):
    n, d = x.shape
    n_out = w.shape[1]
    table_tiles = cos_t.shape[0] // tm
    return pl.pallas_call(
        functools.partial(_proj_rope_kernel, rope_cols=rope_cols),
        grid=(n // tm,),
        in_specs=[pl.BlockSpec((tm, d), lambda i: (i, 0)), _const_spec((1, d)), _const_spec((d, n_out)),
                  pl.BlockSpec((tm, V7X_LANES), lambda i: (i % table_tiles, 0)),
                  pl.BlockSpec((tm, V7X_LANES), lambda i: (i % table_tiles, 0))],
        out_specs=pl.BlockSpec((tm, n_out), lambda i: (i, 0)),
        out_shape=jax.ShapeDtypeStruct((n, n_out), out_dtype),
        compiler_params=_params("parallel"),
        name="proj_rope",
    )(x, nw, w, cos_t, sin_t)


def rope_tables(pos):
    half = ROT_DIM // 2
    inv_freq = ROPE_THETA ** (-jnp.arange(half, dtype=F32) / half)
    ang = pos.astype(F32)[:, None] * inv_freq[None, :]
    cos, sin = jnp.cos(ang), jnp.sin(ang)
    ones, zeros = jnp.ones_like(cos), jnp.zeros_like(sin)
    pad = (HEAD_DIM - ROT_DIM) // half
    cos_h = jnp.concatenate([cos, cos] + [ones] * pad, axis=1)
    sin_h = jnp.concatenate([-sin, sin] + [zeros] * pad, axis=1)
    reps = V7X_LANES // HEAD_DIM
    return jnp.tile(cos_h, (1, reps)), jnp.tile(sin_h, (1, reps))


def _head_masks(shape):
    lane = lax.broadcasted_iota(jnp.int32, shape, 1)
    return [(lane // HEAD_DIM) == h for h in range(HEADS_PER_GROUP)]


def _attn_prompt_kernel(q_ref, kc_ref, kp_ref, vc_ref, vp_ref, o_ref, l_ref, *, n_keys):
    i = pl.program_id(2)
    tq = q_ref.shape[0]
    q = q_ref[...]
    k = jnp.concatenate([kp_ref[...], kc_ref[...]], axis=0).astype(BF16)
    v = jnp.concatenate([vp_ref[...], vc_ref[...]], axis=0).astype(BF16)
    hm_q = _head_masks(q.shape)
    hm_v = _head_masks(v.shape)
    qs = jnp.concatenate([jnp.where(m, q, jnp.zeros_like(q)) for m in hm_q], axis=0)
    s = _dot_nt(qs, k) * (HEAD_DIM ** -0.5)
    qi = lax.broadcasted_iota(jnp.int32, s.shape, 0) % tq
    kj = lax.broadcasted_iota(jnp.int32, s.shape, 1)
    rel = qi + tq - kj
    ok = (rel >= 0) & (rel <= n_keys) & ((kj >= tq) | (i > 0))
    s = jnp.where(ok, s, -jnp.inf)
    m = jnp.max(s, axis=-1, keepdims=True)
    p = jnp.exp(s - m)
    l = jnp.sum(p, axis=-1, keepdims=True)
    pn = (p / l).astype(BF16)
    lse = m + jnp.log(l)
    o = jnp.zeros(o_ref.shape, F32)
    lmap = jnp.zeros(l_ref.shape, F32)
    for h in range(HEADS_PER_GROUP):
        rows = slice(h * tq, (h + 1) * tq)
        o = o + _dot(pn[rows], jnp.where(hm_v[h], v, jnp.zeros_like(v)))
        lmap = jnp.where(hm_q[h], lse[rows], lmap)
    o_ref[...] = o
    l_ref[...] = lmap


def attn_prompt(q, kv, group, batch, seq_len):
    window, dil = GROUPS[group]
    n_keys = window // dil
    tq = n_keys
    L = seq_len // dil
    nblk = L // tq
    n_attn = N_GROUPS * GROUP_COLS
    qv = q.reshape(batch, L, dil * n_attn)
    kvv = kv.reshape(batch, L, dil * 2 * n_attn)
    blk = (None, tq, GROUP_COLS)
    qcol = lambda b, r, i: (b, i, r * N_GROUPS + group)
    kcol = lambda b, r, i: (b, i, r * 2 * N_GROUPS + group)
    kcol_prev = lambda b, r, i: (b, jnp.maximum(i - 1, 0), r * 2 * N_GROUPS + group)
    vcol = lambda b, r, i: (b, i, r * 2 * N_GROUPS + N_GROUPS + group)
    vcol_prev = lambda b, r, i: (b, jnp.maximum(i - 1, 0), r * 2 * N_GROUPS + N_GROUPS + group)
    ocol = lambda b, r, i: (b, i, r)
    o, lmap = pl.pallas_call(
        functools.partial(_attn_prompt_kernel, n_keys=n_keys),
        grid=(batch, dil, nblk),
        in_specs=[pl.BlockSpec(blk, qcol), pl.BlockSpec(blk, kcol), pl.BlockSpec(blk, kcol_prev),
                  pl.BlockSpec(blk, vcol), pl.BlockSpec(blk, vcol_prev)],
        out_specs=[pl.BlockSpec(blk, ocol), pl.BlockSpec(blk, ocol)],
        out_shape=[jax.ShapeDtypeStruct((batch, L, dil * GROUP_COLS), F32)] * 2,
        compiler_params=_params("parallel", "parallel", "parallel"),
        name=f"attn_prompt_g{group}",
    )(qv, kvv, kvv, kvv, kvv)
    return o.reshape(batch * seq_len, GROUP_COLS), lmap.reshape(batch * seq_len, GROUP_COLS)


def _attn_sample_kernel(q_ref, kvn_ref, c0_ref, c1_ref, c2_ref, o_ref, l_ref, k_all, v_all, *, n_new):
    nq = q_ref.shape[0]
    n_attn = N_GROUPS * GROUP_COLS
    for g, c_ref in enumerate((c0_ref, c1_ref, c2_ref)):
        window, dil = GROUPS[g]
        buf = c_ref.shape[0]
        n_all = buf + V7X_LANES
        gc = slice(g * GROUP_COLS, (g + 1) * GROUP_COLS)
        k_all[0:buf, :] = c_ref[:, 0:GROUP_COLS]
        v_all[0:buf, :] = c_ref[:, GROUP_COLS:2 * GROUP_COLS]
        k_all[buf:n_all, :] = jnp.zeros((V7X_LANES, GROUP_COLS), F32)
        v_all[buf:n_all, :] = jnp.zeros((V7X_LANES, GROUP_COLS), F32)
        k_all[buf:buf + n_new, :] = kvn_ref[:, gc]
        v_all[buf:buf + n_new, :] = kvn_ref[:, n_attn + g * GROUP_COLS:n_attn + (g + 1) * GROUP_COLS]
        k = k_all[0:n_all, :].astype(BF16)
        v = v_all[0:n_all, :].astype(BF16)
        q = q_ref[:, gc]
        t = lax.broadcasted_iota(jnp.int32, (nq, n_all), 0)
        key = lax.broadcasted_iota(jnp.int32, (nq, n_all), 1)
        rel = buf + t - key
        ok = (rel >= 0) & ((rel & (dil - 1)) == 0) & (rel <= window) & (key < buf + n_new)
        hm_q = _head_masks(q.shape)
        hm_v = _head_masks(v.shape)
        o = jnp.zeros((nq, GROUP_COLS), F32)
        lmap = jnp.zeros((nq, GROUP_COLS), F32)
        for h in range(HEADS_PER_GROUP):
            s = _dot_nt(jnp.where(hm_q[h], q, jnp.zeros_like(q)), k) * (HEAD_DIM ** -0.5)
            s = jnp.where(ok, s, -jnp.inf)
            m = jnp.max(s, axis=-1, keepdims=True)
            p = jnp.exp(s - m)
            l = jnp.sum(p, axis=-1, keepdims=True)
            o = o + _dot((p / l).astype(BF16), jnp.where(hm_v[h], v, jnp.zeros_like(v)))
            lmap = jnp.where(hm_q[h], m + jnp.log(l), lmap)
        o_ref[:, gc] = o
        l_ref[:, gc] = lmap


def attn_sample(q, kv_new, caches, n_new):
    nb, nq, n_attn = q.shape
    max_buf = max(c.shape[1] for c in caches)
    in_specs = [pl.BlockSpec((None, nq, n_attn), lambda b: (b, 0, 0)),
                pl.BlockSpec((None, n_new, 2 * n_attn), lambda b: (b, 0, 0))]
    in_specs += [pl.BlockSpec((None, c.shape[1], 2 * GROUP_COLS), lambda b: (b, 0, 0)) for c in caches]
    return pl.pallas_call(
        functools.partial(_attn_sample_kernel, n_new=n_new),
        grid=(nb,),
        in_specs=in_specs,
        out_specs=[pl.BlockSpec((None, nq, n_attn), lambda b: (b, 0, 0))] * 2,
        out_shape=[jax.ShapeDtypeStruct((nb, nq, n_attn), F32)] * 2,
        scratch_shapes=[pltpu.VMEM((max_buf + V7X_LANES, GROUP_COLS), F32)] * 2,
        compiler_params=_params("parallel"),
        name="attn_sample",
    )(q, kv_new, *caches)


def _combine_wo_kernel(o0_ref, o1_ref, o2_ref, l0_ref, l1_ref, l2_ref, x_ref, w_ref, out_ref):
    ls = [l0_ref[...], l1_ref[...], l2_ref[...]]
    mx = jnp.maximum(jnp.maximum(ls[0], ls[1]), ls[2])
    es = [jnp.exp(l - mx) for l in ls]
    tot = es[0] + es[1] + es[2]
    os_ = [o0_ref[...], o1_ref[...], o2_ref[...]]
    acc = x_ref[...]
    for g in range(N_GROUPS):
        og = (os_[g] * (es[g] / tot)).astype(BF16)
        acc = acc + _dot(og, w_ref[g * GROUP_COLS:(g + 1) * GROUP_COLS, :])
    out_ref[...] = acc


def combine_wo(os_, ls, x, w_o, tm):
    n, d = x.shape
    gspec = pl.BlockSpec((tm, GROUP_COLS), lambda i: (i, 0))
    return pl.pallas_call(
        _combine_wo_kernel,
        grid=(n // tm,),
        in_specs=[gspec] * 6 + [pl.BlockSpec((tm, d), lambda i: (i, 0)), _const_spec(w_o.shape)],
        out_specs=pl.BlockSpec((tm, d), lambda i: (i, 0)),
        out_shape=jax.ShapeDtypeStruct((n, d), F32),
        compiler_params=_params("parallel"),
        name="combine_wo",
    )(*os_, *ls, x, w_o)


def _router_kernel(x_ref, nw_ref, wr_ref, xn_ref, meta_ref, metat_ref, cnt_ref, *, n_experts):
    xn = _rms(x_ref[...], nw_ref[...])
    xn_ref[...] = xn.astype(BF16)
    xh = xn.astype(BF16)
    xl = (xn - xh.astype(F32)).astype(BF16)
    wr = wr_ref[...]
    wh = wr.astype(BF16)
    wl = (wr - wh.astype(F32)).astype(BF16)
    logits = _dot(xh, wh) + (_dot(xl, wh) + _dot(xh, wl))
    lane = lax.broadcasted_iota(jnp.int32, logits.shape, 1)
    logits = jnp.where(lane < n_experts, logits, -jnp.inf)
    v1 = jnp.max(logits, axis=-1, keepdims=True)
    i1 = jnp.min(jnp.where(logits == v1, lane, V7X_LANES), axis=-1, keepdims=True)
    rest = jnp.where(lane == i1, -jnp.inf, logits)
    v2 = jnp.max(rest, axis=-1, keepdims=True)
    i2 = jnp.min(jnp.where(rest == v2, lane, V7X_LANES), axis=-1, keepdims=True)
    e2 = jnp.exp(v2 - v1)
    den = 1.0 + e2
    g1, g2 = 1.0 / den, e2 / den
    onehot = ((lane == i1) | (lane == i2)).astype(BF16)
    tm = onehot.shape[0]
    tri = (lax.broadcasted_iota(jnp.int32, (tm, tm), 0) >= lax.broadcasted_iota(jnp.int32, (tm, tm), 1)).astype(BF16)
    csum = _dot(tri, onehot)
    r1 = jnp.sum(jnp.where(lane == i1, csum, 0.0), axis=-1, keepdims=True) - 1.0
    r2 = jnp.sum(jnp.where(lane == i2, csum, 0.0), axis=-1, keepdims=True) - 1.0
    cnt_ref[...] = csum[tm - 1:tm, :].astype(jnp.int32)
    cols = (i1.astype(F32), i2.astype(F32), r1, r2, g1, g2)
    meta = jnp.zeros(logits.shape, F32)
    for ci, col in enumerate(cols):
        meta = jnp.where(lane == ci, col, meta)
    meta_ref[...] = meta[:, :META_COLS]
    metat_ref[...] = meta.T[:META_COLS, :]


def router(x, nw, w_router, tm):
    n, d = x.shape
    n_experts = w_router.shape[1]
    wr = jnp.pad(w_router, ((0, 0), (0, V7X_LANES - n_experts)))
    return pl.pallas_call(
        functools.partial(_router_kernel, n_experts=n_experts),
        grid=(n // tm,),
        in_specs=[pl.BlockSpec((tm, d), lambda i: (i, 0)), _const_spec((1, d)), _const_spec((d, V7X_LANES))],
        out_specs=[pl.BlockSpec((tm, d), lambda i: (i, 0)),
                   pl.BlockSpec((tm, META_COLS), lambda i: (i, 0)),
                   pl.BlockSpec((None, META_COLS, tm), lambda i: (i, 0, 0)),
                   pl.BlockSpec((None, 1, V7X_LANES), lambda i: (i, 0, 0))],
        out_shape=[jax.ShapeDtypeStruct((n, d), BF16),
                   jax.ShapeDtypeStruct((n, META_COLS), F32),
                   jax.ShapeDtypeStruct((n // tm, META_COLS, tm), F32),
                   jax.ShapeDtypeStruct((n // tm, 1, V7X_LANES), jnp.int32)],
        compiler_params=_params("parallel"),
        name="router",
    )(x, nw, wr)


def _segment_bits(tm):
    bits, b = [], SEG_ALIGN
    while b <= tm:
        bits.append(b)
        b *= 2
    return bits[::-1]


def _compact_rows(tm, n_experts):
    worst = TOP_K * tm + n_experts * (SEG_ALIGN - 1)
    return -(-worst // BF16_SUBLANES) * BF16_SUBLANES


def _segment_copies(seg_sm, base_sm, tile, n_experts, tm, hbm_ref, vmem_ref, sems, to_hbm):
    out = []
    local = 0
    for e in range(n_experts):
        n = seg_sm[tile * n_experts + e]
        base = base_sm[tile * n_experts + e]
        for b, bit in enumerate(_segment_bits(tm)):
            done = n & ~(2 * bit - 1)
            src = vmem_ref.at[pl.ds(pl.multiple_of(local + done, SEG_ALIGN), bit)]
            dst = hbm_ref.at[pl.ds(pl.multiple_of(base + done, SEG_ALIGN), bit)]
            if not to_hbm:
                src, dst = dst, src
            out.append(((n & bit) != 0, pltpu.make_async_copy(src, dst, sems.at[e, b])))
        local = local + n
    return out, local


def _token_dest(e_k, r_k, seg_sm, tile, n_experts):
    dest = r_k
    local = 0
    for e in range(n_experts):
        dest = dest + jnp.where(e_k == e, local, 0)
        local = local + seg_sm[tile * n_experts + e]
    return dest


def _dispatch_kernel(seg_sm, base_sm, xn_ref, metat_ref, xs_in, xs_hbm, comp_ref, sems, *, n_experts):
    del xs_in
    i = pl.program_id(0)
    tm, d = xn_ref.shape
    rows = comp_ref.shape[0]
    mt = metat_ref[...]
    as_int = lambda v: v.astype(jnp.int32)
    dest1 = _token_dest(as_int(mt[0:1, :]), as_int(mt[2:3, :]), seg_sm, i, n_experts)
    dest2 = _token_dest(as_int(mt[1:2, :]), as_int(mt[3:4, :]), seg_sm, i, n_experts)
    row_id = lax.broadcasted_iota(jnp.int32, (rows, tm), 0)
    p1, p2 = row_id == dest1, row_id == dest2
    comp_ref[:, 0:d] = _dot((p1 | p2).astype(BF16), xn_ref[...])
    gate = jnp.sum(jnp.where(p1, mt[4:5, :], 0.0) + jnp.where(p2, mt[5:6, :], 0.0), axis=-1, keepdims=True)
    comp_ref[:, d:d + V7X_LANES] = jnp.broadcast_to(gate, (rows, V7X_LANES))
    copies, _ = _segment_copies(seg_sm, base_sm, i, n_experts, tm, xs_hbm, comp_ref, sems, to_hbm=True)
    for cond, cp in copies:
        pl.when(cond)(cp.start)
    for cond, cp in copies:
        pl.when(cond)(cp.wait)


def dispatch(xn, metat, seg, base, total_rows, tm):
    n, d = xn.shape
    n_tiles = n // tm
    n_experts = seg.shape[0] // n_tiles
    rows = _compact_rows(tm, n_experts)
    xs0 = jnp.zeros((total_rows, d + V7X_LANES), F32)
    grid_spec = pltpu.PrefetchScalarGridSpec(
        num_scalar_prefetch=2,
        grid=(n_tiles,),
        in_specs=[pl.BlockSpec((tm, d), lambda i, *_: (i, 0)),
                  pl.BlockSpec((None, META_COLS, tm), lambda i, *_: (i, 0, 0)),
                  pl.BlockSpec(memory_space=pl.ANY)],
        out_specs=pl.BlockSpec(memory_space=pl.ANY),
        scratch_shapes=[pltpu.VMEM((rows, d + V7X_LANES), F32),
                        pltpu.SemaphoreType.DMA((n_experts, len(_segment_bits(tm))))],
    )
    return pl.pallas_call(
        functools.partial(_dispatch_kernel, n_experts=n_experts),
        grid_spec=grid_spec,
        out_shape=jax.ShapeDtypeStruct(xs0.shape, F32),
        input_output_aliases={4: 0},
        compiler_params=_params("arbitrary"),
        name="moe_dispatch",
    )(seg, base, xn, metat, xs0)


def _expert_ffn_kernel(te_sm, nu_sm, xs_ref, wg_ref, wu_ref, wd_ref, ys_ref, xb_ref, acc_ref):
    del te_sm
    j, c = pl.program_id(0), pl.program_id(1)
    last_c = pl.num_programs(1) - 1
    d = xb_ref.shape[1]
    used = j < nu_sm[0]

    @pl.when(used & (c == 0))
    def _():
        xb_ref[...] = xs_ref[:, 0:d].astype(BF16)
        acc_ref[...] = jnp.zeros_like(acc_ref)

    @pl.when(used)
    def _():
        xb = xb_ref[...]
        h = _silu(_dot(xb, wg_ref[...])) * _dot(xb, wu_ref[...])
        acc_ref[...] += _dot(h.astype(BF16), wd_ref[...])

    @pl.when(used & (c == last_c))
    def _():
        ys_ref[...] = xs_ref[:, d:d + 1] * acc_ref[...]

    @pl.when(jnp.logical_not(used) & (c == last_c))
    def _():
        ys_ref[...] = jnp.zeros_like(ys_ref)


def expert_ffn(xs, tile_expert, n_used, wg, wu, wd, tmg):
    total_rows, dx = xs.shape
    n_experts, d, de = wg.shape
    tf = _ff_chunk(de)
    last_used = lambda j, nu: jnp.minimum(j, nu[0] - 1)
    grid_spec = pltpu.PrefetchScalarGridSpec(
        num_scalar_prefetch=2,
        grid=(total_rows // tmg, de // tf),
        in_specs=[pl.BlockSpec((tmg, dx), lambda j, c, te, nu: (last_used(j, nu), 0)),
                  pl.BlockSpec((None, d, tf), lambda j, c, te, nu: (te[j], 0, c)),
                  pl.BlockSpec((None, d, tf), lambda j, c, te, nu: (te[j], 0, c)),
                  pl.BlockSpec((None, tf, d), lambda j, c, te, nu: (te[j], c, 0))],
        out_specs=pl.BlockSpec((tmg, d), lambda j, c, te, nu: (j, 0)),
        scratch_shapes=[pltpu.VMEM((tmg, d), BF16), pltpu.VMEM((tmg, d), F32)],
    )
    return pl.pallas_call(
        _expert_ffn_kernel,
        grid_spec=grid_spec,
        out_shape=jax.ShapeDtypeStruct((total_rows, d), F32),
        compiler_params=_params("arbitrary", "arbitrary"),
        name="moe_expert_ffn",
    )(tile_expert, n_used, xs, wg, wu, wd)


def _combine_kernel(seg_sm, base_sm, x_ref, meta_ref, nf_ref, ys_hbm, y_ref, comp_ref, sems, *, n_experts):
    i = pl.program_id(0)
    tm, d = x_ref.shape
    rows = comp_ref.shape[0]
    copies, n_rows = _segment_copies(seg_sm, base_sm, i, n_experts, tm, ys_hbm, comp_ref, sems, to_hbm=False)
    for cond, cp in copies:
        pl.when(cond)(cp.start)
    meta = meta_ref[...]
    as_int = lambda v: v.astype(jnp.int32)
    dest1 = _token_dest(as_int(meta[:, 0:1]), as_int(meta[:, 2:3]), seg_sm, i, n_experts)
    dest2 = _token_dest(as_int(meta[:, 1:2]), as_int(meta[:, 3:4]), seg_sm, i, n_experts)
    col_id = lax.broadcasted_iota(jnp.int32, (tm, rows), 1)
    pick = ((col_id == dest1) | (col_id == dest2)).astype(BF16)
    for cond, cp in copies:
        pl.when(cond)(cp.wait)
    live = lax.broadcasted_iota(jnp.int32, (rows, 1), 0) < n_rows
    ys = jnp.where(live, comp_ref[...], 0.0)
    hi = ys.astype(BF16)
    lo = (ys - hi.astype(F32)).astype(BF16)
    y_ref[...] = _rms(x_ref[...] + (_dot(pick, hi) + _dot(pick, lo)), nf_ref[...])


def combine_final(x, meta, nf, ys, seg, base, tm):
    n, d = x.shape
    n_tiles = n // tm
    n_experts = seg.shape[0] // n_tiles
    rows = _compact_rows(tm, n_experts)
    grid_spec = pltpu.PrefetchScalarGridSpec(
        num_scalar_prefetch=2,
        grid=(n_tiles,),
        in_specs=[pl.BlockSpec((tm, d), lambda i, *_: (i, 0)),
                  pl.BlockSpec((tm, META_COLS), lambda i, *_: (i, 0)),
                  pl.BlockSpec((1, d), lambda i, *_: (0, 0)),
                  pl.BlockSpec(memory_space=pl.ANY)],
        out_specs=pl.BlockSpec((tm, d), lambda i, *_: (i, 0)),
        scratch_shapes=[pltpu.VMEM((rows, d), F32),
                        pltpu.SemaphoreType.DMA((n_experts, len(_segment_bits(tm))))],
    )
    return pl.pallas_call(
        functools.partial(_combine_kernel, n_experts=n_experts),
        grid_spec=grid_spec,
        out_shape=jax.ShapeDtypeStruct((n, d), F32),
        compiler_params=_params("arbitrary"),
        name="moe_combine",
    )(seg, base, x, meta, nf, ys)


def _round_up(v, m):
    return (v + m - 1) // m * m


def moe_final(x, nw, nf, w_router, wg, wu, wd, tm, tmg):
    n, d = x.shape
    n_experts = w_router.shape[1]
    n_tiles = n // tm
    xn, meta, metat, cnt = router(x, nw, w_router, tm)
    seg = _round_up(cnt[:, 0, :n_experts], SEG_ALIGN)
    within = jnp.cumsum(seg, axis=0) - seg
    exp_rows = _round_up(jnp.sum(seg, axis=0), tmg)
    exp_end = jnp.cumsum(exp_rows)
    base = (exp_end - exp_rows)[None, :] + within
    total_rows = _round_up(TOP_K * n + n_tiles * n_experts * (SEG_ALIGN - 1) + n_experts * (tmg - SEG_ALIGN), tmg)
    n_used = (exp_end[-1] // tmg).reshape(1).astype(jnp.int32)
    tile_start = jnp.arange(total_rows // tmg, dtype=jnp.int32) * tmg
    tile_expert = jnp.minimum(jnp.sum(tile_start[:, None] >= exp_end[None, :], axis=1), n_experts - 1).astype(jnp.int32)
    seg_flat = seg.reshape(-1).astype(jnp.int32)
    base_flat = base.reshape(-1).astype(jnp.int32)
    xs = dispatch(xn, metat, seg_flat, base_flat, total_rows, tm)
    ys = expert_ffn(xs, tile_expert, n_used, wg, wu, wd, tmg)
    return combine_final(x, meta, nf, ys, seg_flat, base_flat, tm)


def kernel(x_prompt, x_sample, state_conv, cache_kv_g0, cache_kv_g1, cache_kv_g2, norm_mix, norm_ffn, norm_kv, norm_final, w_pw1, b_pw1, w_dw, b_dw, ln_conv_g, ln_conv_b, w_pw2, b_pw2, w_q, w_kv, w_o, w_gate_dense, w_up_dense, w_down_dense, w_router, w_gate_exp, w_up_exp, w_down_exp):
    batch, seq_len, d = x_prompt.shape
    dec_batch, dec_seq, _ = x_sample.shape
    caches = (cache_kv_g0, cache_kv_g1, cache_kv_g2)
    n_attn = N_GROUPS * GROUP_COLS
    assert norm_mix.shape[0] == 2 and state_conv.shape[0] == 1 and w_q.shape[0] == 1
    assert all(c.shape[1] == w for c, (w, _) in zip(caches, GROUPS))
    assert dec_seq <= SAMPLE_Q_ROWS

    row = lambda v: v.reshape(1, -1).astype(F32)
    bf = lambda w: w.astype(BF16)
    n_p = batch * seq_len
    n_s = dec_batch * dec_seq
    tm_p = _row_tile(seq_len, 512)
    tm_s = n_s

    wdw = jnp.pad(w_dw[0], ((0, TAP_ROWS - CONV_WIDTH), (0, 0)))
    conv_w = (wdw, row(b_dw[0]), row(ln_conv_g[0]), row(ln_conv_b[0]), bf(w_pw2[0]), row(b_pw2[0]))
    w_pw1_b, b_pw1_r = bf(w_pw1[0]), row(b_pw1[0])
    ffn_w = (bf(w_gate_dense[0]), bf(w_up_dense[0]), bf(w_down_dense[0]))
    w_kv_b, w_q_b, w_o_b = bf(w_kv), bf(w_q[0]), bf(w_o[0])
    moe_w = (bf(w_gate_exp[0]), bf(w_up_exp[0]), bf(w_down_exp[0]))

    cos_p, sin_p = rope_tables(jnp.arange(seq_len))
    cos_s, sin_s = rope_tables(PAST_LEN + jnp.arange(n_s) // dec_batch)

    hp = x_prompt.reshape(n_p, d)
    hs = x_sample.transpose(1, 0, 2).reshape(n_s, d)
    g_p = pw1_glu(hp, row(norm_mix[0]), w_pw1_b, b_pw1_r, tm_p)
    g_s = pw1_glu(hs, row(norm_mix[0]), w_pw1_b, b_pw1_r, tm_s)
    hp = conv_prompt(g_p, hp, seq_len, *conv_w, tm_p)
    full_s = jnp.concatenate([state_conv[0].transpose(1, 0, 2), g_s.reshape(dec_seq, dec_batch, d)], axis=0)
    hs = conv_sample(full_s, hs, *conv_w)
    conv_prompt_out = g_p.reshape(batch, seq_len, d)[:, seq_len - (CONV_WIDTH - 1):][None]
    conv_sample_out = full_s[dec_seq:].transpose(1, 0, 2)[None]

    hp = ffn(hp, row(norm_ffn[0]), *ffn_w, tm_p)
    hs = ffn(hs, row(norm_ffn[0]), *ffn_w, tm_s)

    kv_p = proj_rope(hp, row(norm_kv), w_kv_b, cos_p, sin_p, n_attn, F32, tm_p)
    kv_s = proj_rope(hs, row(norm_kv), w_kv_b, cos_s, sin_s, n_attn, F32, tm_s)
    q_p = proj_rope(hp, row(norm_mix[1]), w_q_b, cos_p, sin_p, n_attn, BF16, tm_p)
    q_s = proj_rope(hs, row(norm_mix[1]), w_q_b, cos_s, sin_s, n_attn, BF16, tm_s)

    att_p = [attn_prompt(q_p, kv_p, g, batch, seq_len) for g in range(N_GROUPS)]
    hp = combine_wo([a[0] for a in att_p], [a[1] for a in att_p], hp, w_o_b, tm_p)

    to_bm = lambda a: a.reshape(dec_seq, dec_batch, -1).transpose(1, 0, 2)
    hs = to_bm(hs).reshape(n_s, d)
    kv_s = to_bm(kv_s)
    q_s = jnp.pad(to_bm(q_s), ((0, 0), (0, SAMPLE_Q_ROWS - dec_seq), (0, 0)))
    caches2 = [c.reshape(dec_batch, c.shape[1], 2 * GROUP_COLS) for c in caches]
    o_s, l_s = attn_sample(q_s, kv_s, caches2, dec_seq)
    grp = lambda a, g: a[:, :dec_seq, g * GROUP_COLS:(g + 1) * GROUP_COLS].reshape(n_s, GROUP_COLS)
    hs = combine_wo([grp(o_s, g) for g in range(N_GROUPS)], [grp(l_s, g) for g in range(N_GROUPS)], hs, w_o_b, tm_s)

    y_p = moe_final(hp, row(norm_ffn[1]), row(norm_final), w_router[0], *moe_w, tm_p, MOE_ROW_TILE)
    y_s = moe_final(hs, row(norm_ffn[1]), row(norm_final), w_router[0], *moe_w, tm_s, tm_s)

    kv_p5 = kv_p.reshape(batch, seq_len, 2, N_GROUPS * HEADS_PER_GROUP, HEAD_DIM)
    kv_s5 = kv_s.reshape(dec_batch, dec_seq, 2, N_GROUPS * HEADS_PER_GROUP, HEAD_DIM)
    kv_out = []
    for g, (window, _) in enumerate(GROUPS):
        keep = min(window, seq_len)
        heads = slice(g * HEADS_PER_GROUP, (g + 1) * HEADS_PER_GROUP)
        kv_out += [kv_p5[:, seq_len - keep:, :, heads], kv_s5[:, :, :, heads]]
    return (y_p.reshape(batch, seq_len, d), y_s.reshape(dec_batch, dec_seq, d),
            conv_prompt_out, conv_sample_out, *kv_out)
```

```python
import functools

import jax
import jax.numpy as jnp
from jax import lax
from jax.experimental import pallas as pl
from jax.experimental.pallas import tpu as pltpu

F32 = jnp.float32
BF16 = jnp.bfloat16

EPS = 1e-5
CONV_WIDTH = 31
HEAD_DIM = 64
ROT_DIM = HEAD_DIM // 4
HEADS_PER_GROUP = 4
GROUP_COLS = HEADS_PER_GROUP * HEAD_DIM
GROUPS = ((128, 1), (512, 4), (2048, 16))
N_GROUPS = len(GROUPS)
N_ATTN = N_GROUPS * GROUP_COLS
ROPE_THETA = 500000.0
TOP_K = 2
PAST_LEN = 16384

V7X_LANES = 128
V7X_SUBLANES = 8
V7X_VMEM_BYTES = 64 * 2**20
VMEM_LIMIT_BYTES = V7X_VMEM_BYTES - 8 * 2**20
BF16_SUBLANES = 2 * V7X_SUBLANES

HALO_ROWS = 32
TAP_ROWS = 32
CONV_ROW_CHUNK = 32
ATTN_Q_ROWS = 512
SAMPLE_Q_ROWS = 8
META_COLS = 8
SEG_ALIGN = V7X_SUBLANES
MOE_ROW_TILE = 1024


def _params(*sem):
    return pltpu.CompilerParams(dimension_semantics=sem, vmem_limit_bytes=VMEM_LIMIT_BYTES)


def _rms_unit(x):
    return x * lax.rsqrt(jnp.mean(x * x, axis=-1, keepdims=True) + EPS)


def _rms(x, w):
    return _rms_unit(x) * w


def _silu(x):
    return x * jax.nn.sigmoid(x)


def _dot(a, b):
    return jnp.dot(a, b, preferred_element_type=F32)


def _dot_nt(a, b):
    return lax.dot_general(a, b, (((1,), (1,)), ((), ())), preferred_element_type=F32)


def _row_tile(n_rows, want):
    t = min(n_rows, want)
    assert n_rows % t == 0 and (t % V7X_SUBLANES == 0 or t == n_rows)
    return t


def _round_up(v, m):
    return (v + m - 1) // m * m


def _const_spec(shape):
    return pl.BlockSpec(shape, lambda *_: (0,) * len(shape), pipeline_mode=pl.Buffered(1))


def _pw1_glu_kernel(x_ref, nw_ref, w_ref, b_ref, g_ref):
    d = x_ref.shape[1]
    xn = _rms(x_ref[...], nw_ref[...]).astype(BF16)
    u = _dot(xn, w_ref[...]) + b_ref[...]
    g_ref[...] = u[:, :d] * jax.nn.sigmoid(u[:, d:])


def pw1_glu(x, nw, w, b, tm):
    n, d = x.shape
    return pl.pallas_call(
        _pw1_glu_kernel,
        grid=(n // tm,),
        in_specs=[pl.BlockSpec((tm, d), lambda i: (i, 0)),
                  _const_spec((1, d)), _const_spec((d, 2 * d)), _const_spec((1, 2 * d))],
        out_specs=pl.BlockSpec((tm, d), lambda i: (i, 0)),
        out_shape=jax.ShapeDtypeStruct((n, d), F32),
        compiler_params=_params("parallel"),
        name="pw1_glu",
    )(x, nw, w, b)


def _conv_tail(c, x, lng, lnb, w2, b2):
    mu = jnp.mean(c, axis=-1, keepdims=True)
    cc = c - mu
    y = cc * lax.rsqrt(jnp.mean(cc * cc, axis=-1, keepdims=True) + EPS)
    y = _silu(y * lng + lnb)
    return x + _dot(y.astype(BF16), w2) + b2


def _conv_prompt_kernel(g_ref, halo_ref, x_ref, wdw_ref, bdw_ref, lng_ref, lnb_ref, w2_ref, b2_ref,
                        o_ref, win_ref, c_ref, *, tiles_per_seq):
    tm, d = g_ref.shape
    first = (pl.program_id(0) % tiles_per_seq) == 0
    win_ref[0:HALO_ROWS, :] = jnp.where(first, 0.0, halo_ref[...])
    win_ref[HALO_ROWS:HALO_ROWS + tm, :] = g_ref[...]
    lead = HALO_ROWS - (CONV_WIDTH - 1)
    span = HALO_ROWS + CONV_ROW_CHUNK

    def chunk(ci, carry):
        r0 = pl.multiple_of(ci * CONV_ROW_CHUNK, CONV_ROW_CHUNK)
        w = win_ref[pl.ds(r0, span), :]
        acc = jnp.broadcast_to(bdw_ref[...], (CONV_ROW_CHUNK, d))
        for phase in range(V7X_SUBLANES):
            ws = w if phase == 0 else pltpu.roll(w, span - phase, axis=0)
            for off in range(phase, lead + CONV_WIDTH, V7X_SUBLANES):
                j = off - lead
                if j >= 0:
                    base = off - phase
                    acc = acc + wdw_ref[j:j + 1, :] * ws[base:base + CONV_ROW_CHUNK, :]
        c_ref[pl.ds(r0, CONV_ROW_CHUNK), :] = acc
        return carry

    lax.fori_loop(0, tm // CONV_ROW_CHUNK, chunk, 0)
    o_ref[...] = _conv_tail(c_ref[...], x_ref[...], lng_ref[...], lnb_ref[...], w2_ref[...], b2_ref[...])


def conv_prompt(g, x, seq_len, wdw, bdw, lng, lnb, w2, b2, tm):
    n, d = g.shape
    tiles_per_seq = seq_len // tm
    halo_per_tile = tm // HALO_ROWS
    return pl.pallas_call(
        functools.partial(_conv_prompt_kernel, tiles_per_seq=tiles_per_seq),
        grid=(n // tm,),
        in_specs=[pl.BlockSpec((tm, d), lambda i: (i, 0)),
                  pl.BlockSpec((HALO_ROWS, d), lambda i: (jnp.maximum(i * halo_per_tile - 1, 0), 0)),
                  pl.BlockSpec((tm, d), lambda i: (i, 0)),
                  _const_spec((TAP_ROWS, d)), _const_spec((1, d)), _const_spec((1, d)), _const_spec((1, d)),
                  _const_spec((d, d)), _const_spec((1, d))],
        out_specs=pl.BlockSpec((tm, d), lambda i: (i, 0)),
        out_shape=jax.ShapeDtypeStruct((n, d), F32),
        scratch_shapes=[pltpu.VMEM((HALO_ROWS + tm, d), F32), pltpu.VMEM((tm, d), F32)],
        compiler_params=_params("parallel"),
        name="conv_prompt",
    )(g, g, x, wdw, bdw, lng, lnb, w2, b2)


def _conv_sample_kernel(full_ref, x_ref, wdw_ref, bdw_ref, lng_ref, lnb_ref, w2_ref, b2_ref, o_ref, *, n_t):
    nb, d = full_ref.shape[1], full_ref.shape[2]
    for t in range(n_t):
        acc = jnp.broadcast_to(bdw_ref[...], (nb, d))
        for j in range(CONV_WIDTH):
            acc = acc + wdw_ref[j:j + 1, :] * full_ref[t + j]
        rows = slice(t * nb, (t + 1) * nb)
        o_ref[rows, :] = _conv_tail(acc, x_ref[rows, :], lng_ref[...], lnb_ref[...], w2_ref[...], b2_ref[...])


def conv_sample(full_tm, x, wdw, bdw, lng, lnb, w2, b2):
    rows_full, nb, d = full_tm.shape
    n_t = rows_full - (CONV_WIDTH - 1)
    n = n_t * nb
    return pl.pallas_call(
        functools.partial(_conv_sample_kernel, n_t=n_t),
        grid=(1,),
        in_specs=[_const_spec((rows_full, nb, d)), _const_spec((n, d)),
                  _const_spec((TAP_ROWS, d)), _const_spec((1, d)), _const_spec((1, d)), _const_spec((1, d)),
                  _const_spec((d, d)), _const_spec((1, d))],
        out_specs=pl.BlockSpec((n, d), lambda i: (0, 0)),
        out_shape=jax.ShapeDtypeStruct((n, d), F32),
        compiler_params=_params("arbitrary"),
        name="conv_sample",
    )(full_tm, x, wdw, bdw, lng, lnb, w2, b2)


def _ffn_kernel(x_ref, nw_ref, wg_ref, wu_ref, wd_ref, o_ref, h_ref, *, ff_chunk):
    ff = wg_ref.shape[1]
    x = x_ref[...]
    xn = _rms(x, nw_ref[...]).astype(BF16)
    for c in range(ff // ff_chunk):
        cols = slice(c * ff_chunk, (c + 1) * ff_chunk)
        h = _silu(_dot(xn, wg_ref[:, cols])) * _dot(xn, wu_ref[:, cols])
        h_ref[:, cols] = h.astype(BF16)
    o_ref[...] = x + _dot(h_ref[...], wd_ref[...])


def _ff_chunk(ff):
    for c in (512, 256, 128):
        if ff % c == 0:
            return c
    return ff


def ffn(x, nw, wg, wu, wd, tm):
    n, d = x.shape
    ff = wg.shape[1]
    return pl.pallas_call(
        functools.partial(_ffn_kernel, ff_chunk=_ff_chunk(ff)),
        grid=(n // tm,),
        in_specs=[pl.BlockSpec((tm, d), lambda i: (i, 0)), _const_spec((1, d)),
                  _const_spec((d, ff)), _const_spec((d, ff)), _const_spec((ff, d))],
        out_specs=pl.BlockSpec((tm, d), lambda i: (i, 0)),
        out_shape=jax.ShapeDtypeStruct((n, d), F32),
        scratch_shapes=[pltpu.VMEM((tm, ff), BF16)],
        compiler_params=_params("parallel"),
        name="ffn_dense",
    )(x, nw, wg, wu, wd)


def _kvq_kernel(x_ref, nkv_ref, nq_ref, wkv_ref, wq_ref, cos_ref, sin_ref, *refs, dils):
    q_refs, kv_refs, slab = refs[0:N_GROUPS], refs[N_GROUPS:2 * N_GROUPS], refs[2 * N_GROUPS]
    tm = x_ref.shape[0]
    xu = _rms_unit(x_ref[...])
    ykv = _dot((xu * nkv_ref[...]).astype(BF16), wkv_ref[...])
    yq = _dot((xu * nq_ref[...]).astype(BF16), wq_ref[...])
    cos_t, sin_t = cos_ref[...], sin_ref[...]
    lane = lax.broadcasted_iota(jnp.int32, cos_t.shape, 1)
    low_half = (lane % HEAD_DIM) < (ROT_DIM // 2)

    def rope(yb):
        partner = jnp.where(low_half,
                            pltpu.roll(yb, V7X_LANES - ROT_DIM // 2, axis=1),
                            pltpu.roll(yb, ROT_DIM // 2, axis=1))
        return yb * cos_t + partner * sin_t

    slab_id = 0
    for g, dil in enumerate(dils):
        n = tm // dil
        for half in range(GROUP_COLS // V7X_LANES):
            c0 = g * GROUP_COLS + half * V7X_LANES
            src = slice(c0, c0 + V7X_LANES)
            vsrc = slice(N_ATTN + c0, N_ATTN + c0 + V7X_LANES)
            pieces = ((q_refs[g], half * V7X_LANES, rope(yq[:, src])),
                      (kv_refs[g], half * V7X_LANES, rope(ykv[:, src])),
                      (kv_refs[g], GROUP_COLS + half * V7X_LANES, ykv[:, vsrc]))
            for out_ref, col, val in pieces:
                cols = slice(col, col + V7X_LANES)
                if dil == 1:
                    out_ref[0, :, cols] = val.astype(out_ref.dtype)
                else:
                    slab[slab_id] = val
                    for r in range(dil):
                        out_ref[r, :, cols] = slab[slab_id, pl.ds(r, n, stride=dil), :].astype(out_ref.dtype)
                    slab_id += 1


def kvq_proj(x, batch, seq_len, nkv, nq, wkv, wq, cos_t, sin_t, dils, tm):
    n, d = x.shape
    tps = seq_len // tm
    assert all(tm % dil == 0 and (tm // dil) % BF16_SUBLANES == 0 for dil in dils)
    n_slabs = max(1, 3 * (GROUP_COLS // V7X_LANES) * sum(dil > 1 for dil in dils))
    out_specs, out_shape = [], []
    for cols, dt in ((GROUP_COLS, BF16), (2 * GROUP_COLS, F32)):
        for dil in dils:
            out_specs.append(pl.BlockSpec((None, dil, tm // dil, cols), lambda b, i: (b, 0, i, 0)))
            out_shape.append(jax.ShapeDtypeStruct((batch, dil, seq_len // dil, cols), dt))
    return pl.pallas_call(
        functools.partial(_kvq_kernel, dils=dils),
        grid=(batch, tps),
        in_specs=[pl.BlockSpec((tm, d), lambda b, i: (b * tps + i, 0)), _const_spec((1, d)), _const_spec((1, d)),
                  _const_spec(wkv.shape), _const_spec(wq.shape),
                  pl.BlockSpec((tm, V7X_LANES), lambda b, i: (i, 0)),
                  pl.BlockSpec((tm, V7X_LANES), lambda b, i: (i, 0))],
        out_specs=out_specs,
        out_shape=out_shape,
        scratch_shapes=[pltpu.VMEM((n_slabs, tm, V7X_LANES), F32)],
        compiler_params=_params("parallel", "parallel"),
        name="kvq_proj",
    )(x, nkv, nq, wkv, wq, cos_t, sin_t)


def rope_tables(pos):
    half = ROT_DIM // 2
    inv_freq = ROPE_THETA ** (-jnp.arange(half, dtype=F32) / half)
    ang = pos.astype(F32)[:, None] * inv_freq[None, :]
    cos, sin = jnp.cos(ang), jnp.sin(ang)
    ones, zeros = jnp.ones_like(cos), jnp.zeros_like(sin)
    pad = (HEAD_DIM - ROT_DIM) // half
    cos_h = jnp.concatenate([cos, cos] + [ones] * pad, axis=1)
    sin_h = jnp.concatenate([-sin, sin] + [zeros] * pad, axis=1)
    reps = V7X_LANES // HEAD_DIM
    return jnp.tile(cos_h, (1, reps)), jnp.tile(sin_h, (1, reps))


def _head_masks(shape):
    lane = lax.broadcasted_iota(jnp.int32, shape, 1)
    return [(lane // HEAD_DIM) == h for h in range(HEADS_PER_GROUP)]


def _attn_prompt_kernel(q_ref, kvc_ref, kvp_ref, o_ref, l_ref, *, n_keys):
    i = pl.program_id(2)
    tq = n_keys
    n_sub = q_ref.shape[0] // tq
    kv = jnp.concatenate([kvp_ref[...], kvc_ref[...]], axis=0)
    k = kv[:, 0:GROUP_COLS].astype(BF16)
    v = kv[:, GROUP_COLS:2 * GROUP_COLS].astype(BF16)
    hm_q = _head_masks((tq, GROUP_COLS))
    hm_v = _head_masks((2 * tq, GROUP_COLS))
    qi = lax.broadcasted_iota(jnp.int32, (tq, 2 * tq), 0)
    kj = lax.broadcasted_iota(jnp.int32, (tq, 2 * tq), 1)
    rel = qi + tq - kj
    band = (rel >= 0) & (rel <= n_keys)
    bias = jnp.where(band, 0.0, -jnp.inf)
    bias_start = jnp.where(band & (kj >= tq), 0.0, -jnp.inf)
    bias_all = jnp.concatenate([bias] * HEADS_PER_GROUP, axis=0)
    bias_first = jnp.where(i == 0, jnp.concatenate([bias_start] * HEADS_PER_GROUP, axis=0), bias_all)
    for jb in range(n_sub):
        q = q_ref[jb * tq:(jb + 1) * tq, :]
        kb = k[jb * tq:(jb + 2) * tq]
        vb = v[jb * tq:(jb + 2) * tq]
        qs = jnp.concatenate([jnp.where(m, q, jnp.zeros_like(q)) for m in hm_q], axis=0)
        s = _dot_nt(qs, kb) * (HEAD_DIM ** -0.5) + (bias_first if jb == 0 else bias_all)
        m = jnp.max(s, axis=-1, keepdims=True)
        p = jnp.exp(s - m)
        l = jnp.sum(p, axis=-1, keepdims=True)
        pn = (p / l).astype(BF16)
        lse = m + jnp.log(l)
        o = jnp.zeros((tq, GROUP_COLS), F32)
        lmap = jnp.zeros((tq, GROUP_COLS), F32)
        for h in range(HEADS_PER_GROUP):
            rows = slice(h * tq, (h + 1) * tq)
            o = o + _dot(pn[rows], jnp.where(hm_v[h], vb, jnp.zeros_like(vb)))
            lmap = jnp.where(hm_q[h], lse[rows], lmap)
        o_ref[jb * tq:(jb + 1) * tq, :] = o
        l_ref[jb * tq:(jb + 1) * tq, :] = lmap


def attn_prompt(q, kv, group):
    window, dil = GROUPS[group]
    n_keys = window // dil
    batch, _, L, _ = q.shape
    tqb = _row_tile(L, ATTN_Q_ROWS)
    assert tqb % n_keys == 0
    per = tqb // n_keys
    cur = lambda b, r, i: (b, r, i, 0)
    prev = lambda b, r, i: (b, r, jnp.maximum(i * per - 1, 0), 0)
    return pl.pallas_call(
        functools.partial(_attn_prompt_kernel, n_keys=n_keys),
        grid=(batch, dil, L // tqb),
        in_specs=[pl.BlockSpec((None, None, tqb, GROUP_COLS), cur),
                  pl.BlockSpec((None, None, tqb, 2 * GROUP_COLS), cur),
                  pl.BlockSpec((None, None, n_keys, 2 * GROUP_COLS), prev)],
        out_specs=[pl.BlockSpec((None, None, tqb, GROUP_COLS), cur)] * 2,
        out_shape=[jax.ShapeDtypeStruct((batch, dil, L, GROUP_COLS), F32)] * 2,
        compiler_params=_params("parallel", "parallel", "parallel"),
        name=f"attn_prompt_g{group}",
    )(q, kv, kv)


def _attn_sample_kernel(*refs, n_new):
    q_refs, kvn_refs, c_refs = refs[0:3], refs[3:6], refs[6:9]
    o_refs, l_refs, k_all, v_all = refs[9:12], refs[12:15], refs[15], refs[16]
    nq = q_refs[0].shape[0]
    for g in range(N_GROUPS):
        window, dil = GROUPS[g]
        c_ref = c_refs[g]
        buf = c_ref.shape[0]
        n_all = buf + V7X_LANES
        k_all[0:buf, :] = c_ref[:, 0:GROUP_COLS]
        v_all[0:buf, :] = c_ref[:, GROUP_COLS:2 * GROUP_COLS]
        k_all[buf:n_all, :] = jnp.zeros((V7X_LANES, GROUP_COLS), F32)
        v_all[buf:n_all, :] = jnp.zeros((V7X_LANES, GROUP_COLS), F32)
        k_all[buf:buf + n_new, :] = kvn_refs[g][:, 0:GROUP_COLS]
        v_all[buf:buf + n_new, :] = kvn_refs[g][:, GROUP_COLS:2 * GROUP_COLS]
        k = k_all[0:n_all, :].astype(BF16)
        v = v_all[0:n_all, :].astype(BF16)
        q = q_refs[g][...]
        t = lax.broadcasted_iota(jnp.int32, (nq, n_all), 0)
        key = lax.broadcasted_iota(jnp.int32, (nq, n_all), 1)
        rel = buf + t - key
        ok = (rel >= 0) & ((rel & (dil - 1)) == 0) & (rel <= window) & (key < buf + n_new)
        hm_q = _head_masks(q.shape)
        hm_v = _head_masks(v.shape)
        o = jnp.zeros((nq, GROUP_COLS), F32)
        lmap = jnp.zeros((nq, GROUP_COLS), F32)
        for h in range(HEADS_PER_GROUP):
            s = _dot_nt(jnp.where(hm_q[h], q, jnp.zeros_like(q)), k) * (HEAD_DIM ** -0.5)
            s = jnp.where(ok, s, -jnp.inf)
            m = jnp.max(s, axis=-1, keepdims=True)
            p = jnp.exp(s - m)
            l = jnp.sum(p, axis=-1, keepdims=True)
            o = o + _dot((p / l).astype(BF16), jnp.where(hm_v[h], v, jnp.zeros_like(v)))
            lmap = jnp.where(hm_q[h], m + jnp.log(l), lmap)
        o_refs[g][...] = o
        l_refs[g][...] = lmap


def attn_sample(qs, kv_news, caches, n_new):
    nb, nq, _ = qs[0].shape
    max_buf = max(c.shape[1] for c in caches)
    per_b = lambda rows, cols: pl.BlockSpec((None, rows, cols), lambda b: (b, 0, 0))
    in_specs = ([per_b(nq, GROUP_COLS)] * N_GROUPS + [per_b(n_new, 2 * GROUP_COLS)] * N_GROUPS
                + [per_b(c.shape[1], 2 * GROUP_COLS) for c in caches])
    outs = pl.pallas_call(
        functools.partial(_attn_sample_kernel, n_new=n_new),
        grid=(nb,),
        in_specs=in_specs,
        out_specs=[per_b(nq, GROUP_COLS)] * (2 * N_GROUPS),
        out_shape=[jax.ShapeDtypeStruct((nb, nq, GROUP_COLS), F32)] * (2 * N_GROUPS),
        scratch_shapes=[pltpu.VMEM((max_buf + V7X_LANES, GROUP_COLS), F32)] * 2,
        compiler_params=_params("parallel"),
        name="attn_sample",
    )(*qs, *kv_news, *caches)
    return outs[0:N_GROUPS], outs[N_GROUPS:2 * N_GROUPS]


def _combine_wo_kernel(*refs, dils):
    o_refs, l_refs = refs[0:N_GROUPS], refs[N_GROUPS:2 * N_GROUPS]
    x_ref, w_ref, out_ref, slab = refs[2 * N_GROUPS:2 * N_GROUPS + 4]
    tm = x_ref.shape[0]
    slab_id = 0

    def in_position_order(ref, dil):
        nonlocal slab_id
        halves = []
        for half in range(GROUP_COLS // V7X_LANES):
            cols = slice(half * V7X_LANES, (half + 1) * V7X_LANES)
            if dil == 1:
                halves.append(ref[0, :, cols])
            else:
                for r in range(dil):
                    slab[slab_id, pl.ds(r, tm // dil, stride=dil), :] = ref[r, :, cols]
                halves.append(slab[slab_id])
                slab_id += 1
        return jnp.concatenate(halves, axis=1)

    ls = [in_position_order(l_refs[g], dils[g]) for g in range(N_GROUPS)]
    mx = jnp.maximum(jnp.maximum(ls[0], ls[1]), ls[2])
    es = [jnp.exp(l - mx) for l in ls]
    tot = es[0] + es[1] + es[2]
    acc = x_ref[...]
    for g in range(N_GROUPS):
        og = (in_position_order(o_refs[g], dils[g]) * (es[g] / tot)).astype(BF16)
        acc = acc + _dot(og, w_ref[g * GROUP_COLS:(g + 1) * GROUP_COLS, :])
    out_ref[...] = acc


def combine_wo(os_, ls, x, batch, seq_len, w_o, dils, tm):
    n, d = x.shape
    tps = seq_len // tm
    gspecs = [pl.BlockSpec((None, dil, tm // dil, GROUP_COLS), lambda b, i: (b, 0, i, 0)) for dil in dils]
    n_slabs = max(1, 2 * (GROUP_COLS // V7X_LANES) * sum(dil > 1 for dil in dils))
    return pl.pallas_call(
        functools.partial(_combine_wo_kernel, dils=dils),
        grid=(batch, tps),
        in_specs=gspecs + gspecs + [pl.BlockSpec((tm, d), lambda b, i: (b * tps + i, 0)), _const_spec(w_o.shape)],
        out_specs=pl.BlockSpec((tm, d), lambda b, i: (b * tps + i, 0)),
        out_shape=jax.ShapeDtypeStruct((n, d), F32),
        scratch_shapes=[pltpu.VMEM((n_slabs, tm, V7X_LANES), F32)],
        compiler_params=_params("parallel", "parallel"),
        name="combine_wo",
    )(*os_, *ls, x, w_o)


def _router_kernel(x_ref, nw_ref, wr_ref, xn_ref, meta_ref, metat_ref, cnt_ref, *, n_experts):
    xn = _rms(x_ref[...], nw_ref[...])
    xn_ref[...] = xn.astype(BF16)
    xh = xn.astype(BF16)
    xl = (xn - xh.astype(F32)).astype(BF16)
    wr = wr_ref[...]
    wh = wr.astype(BF16)
    wl = (wr - wh.astype(F32)).astype(BF16)
    logits = _dot(xh, wh) + (_dot(xl, wh) + _dot(xh, wl))
    lane = lax.broadcasted_iota(jnp.int32, logits.shape, 1)
    logits = jnp.where(lane < n_experts, logits, -jnp.inf)
    v1 = jnp.max(logits, axis=-1, keepdims=True)
    i1 = jnp.min(jnp.where(logits == v1, lane, V7X_LANES), axis=-1, keepdims=True)
    rest = jnp.where(lane == i1, -jnp.inf, logits)
    v2 = jnp.max(rest, axis=-1, keepdims=True)
    i2 = jnp.min(jnp.where(rest == v2, lane, V7X_LANES), axis=-1, keepdims=True)
    e2 = jnp.exp(v2 - v1)
    den = 1.0 + e2
    g1, g2 = 1.0 / den, e2 / den
    onehot = ((lane == i1) | (lane == i2)).astype(BF16)
    tm = onehot.shape[0]
    tri = (lax.broadcasted_iota(jnp.int32, (tm, tm), 0) >= lax.broadcasted_iota(jnp.int32, (tm, tm), 1)).astype(BF16)
    csum = _dot(tri, onehot)
    r1 = jnp.sum(jnp.where(lane == i1, csum, 0.0), axis=-1, keepdims=True) - 1.0
    r2 = jnp.sum(jnp.where(lane == i2, csum, 0.0), axis=-1, keepdims=True) - 1.0
    cnt_ref[...] = csum[tm - 1:tm, :].astype(jnp.int32)
    cols = (i1.astype(F32), i2.astype(F32), r1, r2, g1, g2)
    meta = jnp.zeros(logits.shape, F32)
    for ci, col in enumerate(cols):
        meta = jnp.where(lane == ci, col, meta)
    meta_ref[...] = meta[:, :META_COLS]
    metat_ref[...] = meta.T[:META_COLS, :]


def router(x, nw, w_router, tm):
    n, d = x.shape
    n_experts = w_router.shape[1]
    wr = jnp.pad(w_router, ((0, 0), (0, V7X_LANES - n_experts)))
    return pl.pallas_call(
        functools.partial(_router_kernel, n_experts=n_experts),
        grid=(n // tm,),
        in_specs=[pl.BlockSpec((tm, d), lambda i: (i, 0)), _const_spec((1, d)), _const_spec((d, V7X_LANES))],
        out_specs=[pl.BlockSpec((tm, d), lambda i: (i, 0)),
                   pl.BlockSpec((tm, META_COLS), lambda i: (i, 0)),
                   pl.BlockSpec((None, META_COLS, tm), lambda i: (i, 0, 0)),
                   pl.BlockSpec((None, 1, V7X_LANES), lambda i: (i, 0, 0))],
        out_shape=[jax.ShapeDtypeStruct((n, d), BF16),
                   jax.ShapeDtypeStruct((n, META_COLS), F32),
                   jax.ShapeDtypeStruct((n // tm, META_COLS, tm), F32),
                   jax.ShapeDtypeStruct((n // tm, 1, V7X_LANES), jnp.int32)],
        compiler_params=_params("parallel"),
        name="router",
    )(x, nw, wr)


def _segment_bits(tm):
    bits, b = [], SEG_ALIGN
    while b <= tm:
        bits.append(b)
        b *= 2
    return bits[::-1]


def _compact_rows(tm, n_experts):
    worst = TOP_K * tm + n_experts * (SEG_ALIGN - 1)
    return _round_up(worst, BF16_SUBLANES)


def _segment_copies(seg_sm, base_sm, tile, n_experts, tm, hbm_ref, vmem_ref, sems, to_hbm):
    out = []
    local = 0
    for e in range(n_experts):
        n = seg_sm[tile * n_experts + e]
        base = base_sm[tile * n_experts + e]
        for b, bit in enumerate(_segment_bits(tm)):
            done = n & ~(2 * bit - 1)
            src = vmem_ref.at[pl.ds(pl.multiple_of(local + done, SEG_ALIGN), bit)]
            dst = hbm_ref.at[pl.ds(pl.multiple_of(base + done, SEG_ALIGN), bit)]
            if not to_hbm:
                src, dst = dst, src
            out.append(((n & bit) != 0, pltpu.make_async_copy(src, dst, sems.at[e, b])))
        local = local + n
    return out, local


def _token_dest(e_k, r_k, seg_sm, tile, n_experts):
    dest = r_k
    local = 0
    for e in range(n_experts):
        dest = dest + jnp.where(e_k == e, local, 0)
        local = local + seg_sm[tile * n_experts + e]
    return dest


def _dispatch_kernel(seg_sm, base_sm, xn_ref, metat_ref, xs_in, xs_hbm, comp_ref, sems, *, n_experts):
    del xs_in
    i = pl.program_id(0)
    tm, d = xn_ref.shape
    rows = comp_ref.shape[0]
    mt = metat_ref[...]
    as_int = lambda v: v.astype(jnp.int32)
    dest1 = _token_dest(as_int(mt[0:1, :]), as_int(mt[2:3, :]), seg_sm, i, n_experts)
    dest2 = _token_dest(as_int(mt[1:2, :]), as_int(mt[3:4, :]), seg_sm, i, n_experts)
    row_id = lax.broadcasted_iota(jnp.int32, (rows, tm), 0)
    p1, p2 = row_id == dest1, row_id == dest2
    comp_ref[:, 0:d] = _dot((p1 | p2).astype(BF16), xn_ref[...])
    gate = jnp.sum(jnp.where(p1, mt[4:5, :], 0.0) + jnp.where(p2, mt[5:6, :], 0.0), axis=-1, keepdims=True)
    comp_ref[:, d:d + V7X_LANES] = jnp.broadcast_to(gate, (rows, V7X_LANES))
    copies, _ = _segment_copies(seg_sm, base_sm, i, n_experts, tm, xs_hbm, comp_ref, sems, to_hbm=True)
    for cond, cp in copies:
        pl.when(cond)(cp.start)
    for cond, cp in copies:
        pl.when(cond)(cp.wait)


def dispatch(xn, metat, seg, base, total_rows, tm):
    n, d = xn.shape
    n_tiles = n // tm
    n_experts = seg.shape[0] // n_tiles
    rows = _compact_rows(tm, n_experts)
    xs0 = jnp.zeros((total_rows, d + V7X_LANES), F32)
    grid_spec = pltpu.PrefetchScalarGridSpec(
        num_scalar_prefetch=2,
        grid=(n_tiles,),
        in_specs=[pl.BlockSpec((tm, d), lambda i, *_: (i, 0)),
                  pl.BlockSpec((None, META_COLS, tm), lambda i, *_: (i, 0, 0)),
                  pl.BlockSpec(memory_space=pl.ANY)],
        out_specs=pl.BlockSpec(memory_space=pl.ANY),
        scratch_shapes=[pltpu.VMEM((rows, d + V7X_LANES), F32),
                        pltpu.SemaphoreType.DMA((n_experts, len(_segment_bits(tm))))],
    )
    return pl.pallas_call(
        functools.partial(_dispatch_kernel, n_experts=n_experts),
        grid_spec=grid_spec,
        out_shape=jax.ShapeDtypeStruct(xs0.shape, F32),
        input_output_aliases={4: 0},
        compiler_params=_params("arbitrary"),
        name="moe_dispatch",
    )(seg, base, xn, metat, xs0)


def _expert_ffn_kernel(te_sm, nu_sm, xs_ref, wg_ref, wu_ref, wd_ref, ys_ref, xb_ref, acc_ref):
    del te_sm
    j, c = pl.program_id(0), pl.program_id(1)
    last_c = pl.num_programs(1) - 1
    d = xb_ref.shape[1]
    used = j < nu_sm[0]

    @pl.when(used & (c == 0))
    def _():
        xb_ref[...] = xs_ref[:, 0:d].astype(BF16)
        acc_ref[...] = jnp.zeros_like(acc_ref)

    @pl.when(used)
    def _():
        xb = xb_ref[...]
        h = _silu(_dot(xb, wg_ref[...])) * _dot(xb, wu_ref[...])
        acc_ref[...] += _dot(h.astype(BF16), wd_ref[...])

    @pl.when(used & (c == last_c))
    def _():
        ys_ref[...] = xs_ref[:, d:d + 1] * acc_ref[...]

    @pl.when(jnp.logical_not(used) & (c == last_c))
    def _():
        ys_ref[...] = jnp.zeros_like(ys_ref)


def expert_ffn(xs, tile_expert, n_used, wg, wu, wd, tmg):
    total_rows, dx = xs.shape
    n_experts, d, de = wg.shape
    tf = _ff_chunk(de)
    last_used = lambda j, nu: jnp.minimum(j, nu[0] - 1)
    grid_spec = pltpu.PrefetchScalarGridSpec(
        num_scalar_prefetch=2,
        grid=(total_rows // tmg, de // tf),
        in_specs=[pl.BlockSpec((tmg, dx), lambda j, c, te, nu: (last_used(j, nu), 0)),
                  pl.BlockSpec((None, d, tf), lambda j, c, te, nu: (te[j], 0, c)),
                  pl.BlockSpec((None, d, tf), lambda j, c, te, nu: (te[j], 0, c)),
                  pl.BlockSpec((None, tf, d), lambda j, c, te, nu: (te[j], c, 0))],
        out_specs=pl.BlockSpec((tmg, d), lambda j, c, te, nu: (j, 0)),
        scratch_shapes=[pltpu.VMEM((tmg, d), BF16), pltpu.VMEM((tmg, d), F32)],
    )
    return pl.pallas_call(
        _expert_ffn_kernel,
        grid_spec=grid_spec,
        out_shape=jax.ShapeDtypeStruct((total_rows, d), F32),
        compiler_params=_params("arbitrary", "arbitrary"),
        name="moe_expert_ffn",
    )(tile_expert, n_used, xs, wg, wu, wd)


def _combine_kernel(seg_sm, base_sm, x_ref, meta_ref, nf_ref, ys_hbm, y_ref, comp_ref, sems, *, n_experts):
    i = pl.program_id(0)
    tm, d = x_ref.shape
    rows = comp_ref.shape[0]
    copies, n_rows = _segment_copies(seg_sm, base_sm, i, n_experts, tm, ys_hbm, comp_ref, sems, to_hbm=False)
    for cond, cp in copies:
        pl.when(cond)(cp.start)
    meta = meta_ref[...]
    as_int = lambda v: v.astype(jnp.int32)
    dest1 = _token_dest(as_int(meta[:, 0:1]), as_int(meta[:, 2:3]), seg_sm, i, n_experts)
    dest2 = _token_dest(as_int(meta[:, 1:2]), as_int(meta[:, 3:4]), seg_sm, i, n_experts)
    col_id = lax.broadcasted_iota(jnp.int32, (tm, rows), 1)
    pick = ((col_id == dest1) | (col_id == dest2)).astype(BF16)
    for cond, cp in copies:
        pl.when(cond)(cp.wait)
    live = lax.broadcasted_iota(jnp.int32, (rows, 1), 0) < n_rows
    ys = jnp.where(live, comp_ref[...], 0.0)
    hi = ys.astype(BF16)
    lo = (ys - hi.astype(F32)).astype(BF16)
    y_ref[...] = _rms(x_ref[...] + (_dot(pick, hi) + _dot(pick, lo)), nf_ref[...])


def combine_final(x, meta, nf, ys, seg, base, tm):
    n, d = x.shape
    n_tiles = n // tm
    n_experts = seg.shape[0] // n_tiles
    rows = _compact_rows(tm, n_experts)
    grid_spec = pltpu.PrefetchScalarGridSpec(
        num_scalar_prefetch=2,
        grid=(n_tiles,),
        in_specs=[pl.BlockSpec((tm, d), lambda i, *_: (i, 0)),
                  pl.BlockSpec((tm, META_COLS), lambda i, *_: (i, 0)),
                  pl.BlockSpec((1, d), lambda i, *_: (0, 0)),
                  pl.BlockSpec(memory_space=pl.ANY)],
        out_specs=pl.BlockSpec((tm, d), lambda i, *_: (i, 0)),
        scratch_shapes=[pltpu.VMEM((rows, d), F32),
                        pltpu.SemaphoreType.DMA((n_experts, len(_segment_bits(tm))))],
    )
    return pl.pallas_call(
        functools.partial(_combine_kernel, n_experts=n_experts),
        grid_spec=grid_spec,
        out_shape=jax.ShapeDtypeStruct((n, d), F32),
        compiler_params=_params("arbitrary"),
        name="moe_combine",
    )(seg, base, x, meta, nf, ys)


def moe_final(x, nw, nf, w_router, wg, wu, wd, tm, tmg):
    n, d = x.shape
    n_experts = w_router.shape[1]
    n_tiles = n // tm
    xn, meta, metat, cnt = router(x, nw, w_router, tm)
    seg = _round_up(cnt[:, 0, :n_experts], SEG_ALIGN)
    within = jnp.cumsum(seg, axis=0) - seg
    exp_rows = _round_up(jnp.sum(seg, axis=0), tmg)
    exp_end = jnp.cumsum(exp_rows)
    base = (exp_end - exp_rows)[None, :] + within
    total_rows = _round_up(TOP_K * n + n_tiles * n_experts * (SEG_ALIGN - 1) + n_experts * (tmg - SEG_ALIGN), tmg)
    n_used = (exp_end[-1] // tmg).reshape(1).astype(jnp.int32)
    tile_start = jnp.arange(total_rows // tmg, dtype=jnp.int32) * tmg
    tile_expert = jnp.minimum(jnp.sum(tile_start[:, None] >= exp_end[None, :], axis=1), n_experts - 1).astype(jnp.int32)
    seg_flat = seg.reshape(-1).astype(jnp.int32)
    base_flat = base.reshape(-1).astype(jnp.int32)
    xs = dispatch(xn, metat, seg_flat, base_flat, total_rows, tm)
    ys = expert_ffn(xs, tile_expert, n_used, wg, wu, wd, tmg)
    return combine_final(x, meta, nf, ys, seg_flat, base_flat, tm)


def kernel(x_prompt, x_sample, state_conv, cache_kv_g0, cache_kv_g1, cache_kv_g2, norm_mix, norm_ffn, norm_kv, norm_final, w_pw1, b_pw1, w_dw, b_dw, ln_conv_g, ln_conv_b, w_pw2, b_pw2, w_q, w_kv, w_o, w_gate_dense, w_up_dense, w_down_dense, w_router, w_gate_exp, w_up_exp, w_down_exp):
    batch, seq_len, d = x_prompt.shape
    dec_batch, dec_seq, _ = x_sample.shape
    caches = (cache_kv_g0, cache_kv_g1, cache_kv_g2)
    assert norm_mix.shape[0] == 2 and state_conv.shape[0] == 1 and w_q.shape[0] == 1
    assert all(c.shape[1] == w for c, (w, _) in zip(caches, GROUPS))
    assert dec_seq <= SAMPLE_Q_ROWS

    row = lambda v: v.reshape(1, -1).astype(F32)
    bf = lambda w: w.astype(BF16)
    n_p = batch * seq_len
    n_s = dec_batch * dec_seq
    tm_p = _row_tile(seq_len, 512)
    tm_s = n_s
    dils = tuple(dil for _, dil in GROUPS)
    no_dil = (1,) * N_GROUPS

    wdw = jnp.pad(w_dw[0], ((0, TAP_ROWS - CONV_WIDTH), (0, 0)))
    conv_w = (wdw, row(b_dw[0]), row(ln_conv_g[0]), row(ln_conv_b[0]), bf(w_pw2[0]), row(b_pw2[0]))
    w_pw1_b, b_pw1_r = bf(w_pw1[0]), row(b_pw1[0])
    ffn_w = (bf(w_gate_dense[0]), bf(w_up_dense[0]), bf(w_down_dense[0]))
    kvq_w = (row(norm_kv), row(norm_mix[1]), bf(w_kv), bf(w_q[0]))
    w_o_b = bf(w_o[0])
    moe_w = (bf(w_gate_exp[0]), bf(w_up_exp[0]), bf(w_down_exp[0]))

    cos_p, sin_p = rope_tables(jnp.arange(seq_len))
    cos_s, sin_s = rope_tables(PAST_LEN + jnp.arange(n_s) // dec_batch)

    hp = x_prompt.reshape(n_p, d)
    hs = x_sample.transpose(1, 0, 2).reshape(n_s, d)
    g_p = pw1_glu(hp, row(norm_mix[0]), w_pw1_b, b_pw1_r, tm_p)
    g_s = pw1_glu(hs, row(norm_mix[0]), w_pw1_b, b_pw1_r, tm_s)
    hp = conv_prompt(g_p, hp, seq_len, *conv_w, tm_p)
    full_s = jnp.concatenate([state_conv[0].transpose(1, 0, 2), g_s.reshape(dec_seq, dec_batch, d)], axis=0)
    hs = conv_sample(full_s, hs, *conv_w)
    conv_prompt_out = g_p.reshape(batch, seq_len, d)[:, seq_len - (CONV_WIDTH - 1):][None]
    conv_sample_out = full_s[dec_seq:].transpose(1, 0, 2)[None]

    hp = ffn(hp, row(norm_ffn[0]), *ffn_w, tm_p)
    hs = ffn(hs, row(norm_ffn[0]), *ffn_w, tm_s)

    kvq_p = kvq_proj(hp, batch, seq_len, *kvq_w, cos_p, sin_p, dils, tm_p)
    q_p, kv_p = kvq_p[0:N_GROUPS], kvq_p[N_GROUPS:]
    kvq_s = kvq_proj(hs, 1, n_s, *kvq_w, cos_s, sin_s, no_dil, tm_s)
    to_bm = lambda a: a.reshape(dec_seq, dec_batch, -1).transpose(1, 0, 2)
    q_s = [jnp.pad(to_bm(a), ((0, 0), (0, SAMPLE_Q_ROWS - dec_seq), (0, 0))) for a in kvq_s[0:N_GROUPS]]
    kv_s = [to_bm(a) for a in kvq_s[N_GROUPS:]]

    att_p = [attn_prompt(q_p[g], kv_p[g], g) for g in range(N_GROUPS)]
    hp = combine_wo([a[0] for a in att_p], [a[1] for a in att_p], hp, batch, seq_len, w_o_b, dils, tm_p)

    hs = to_bm(hs).reshape(n_s, d)
    caches2 = [c.reshape(dec_batch, c.shape[1], 2 * GROUP_COLS) for c in caches]
    o_s, l_s = attn_sample(q_s, kv_s, caches2, dec_seq)
    as_rows = lambda a: a[:, :dec_seq].reshape(1, 1, n_s, GROUP_COLS)
    hs = combine_wo([as_rows(a) for a in o_s], [as_rows(a) for a in l_s], hs, 1, n_s, w_o_b, no_dil, tm_s)

    y_p = moe_final(hp, row(norm_ffn[1]), row(norm_final), w_router[0], *moe_w, tm_p, MOE_ROW_TILE)
    y_s = moe_final(hs, row(norm_ffn[1]), row(norm_final), w_router[0], *moe_w, tm_s, tm_s)

    kv_out = []
    for g, (window, dil) in enumerate(GROUPS):
        keep = min(window, seq_len)
        tail = kv_p[g][:, :, (seq_len - keep) // dil:, :]
        kv_out.append(tail.transpose(0, 2, 1, 3).reshape(batch, keep, 2, HEADS_PER_GROUP, HEAD_DIM))
        kv_out.append(kv_s[g].reshape(dec_batch, dec_seq, 2, HEADS_PER_GROUP, HEAD_DIM))
    return (y_p.reshape(batch, seq_len, d), y_s.reshape(dec_batch, dec_seq, d),
            conv_prompt_out, conv_sample_out, *kv_out)
```

```python
import functools

import jax
import jax.numpy as jnp
from jax import lax
from jax.experimental import pallas as pl
from jax.experimental.pallas import tpu as pltpu

F32 = jnp.float32
BF16 = jnp.bfloat16

EPS = 1e-5
CONV_WIDTH = 31
HEAD_DIM = 64
ROT_DIM = HEAD_DIM // 4
HEADS_PER_GROUP = 4
GROUP_COLS = HEADS_PER_GROUP * HEAD_DIM
GROUPS = ((128, 1), (512, 4), (2048, 16))
N_GROUPS = len(GROUPS)
N_ATTN = N_GROUPS * GROUP_COLS
ROPE_THETA = 500000.0
TOP_K = 2
PAST_LEN = 16384

V7X_LANES = 128
V7X_SUBLANES = 8
V7X_VMEM_BYTES = 64 * 2**20
VMEM_LIMIT_BYTES = V7X_VMEM_BYTES - 8 * 2**20
BF16_SUBLANES = 2 * V7X_SUBLANES

HALO_ROWS = 32
TAP_ROWS = 32
CONV_ROW_CHUNK = 32
ATTN_Q_ROWS = 512
SAMPLE_Q_ROWS = 8
META_COLS = 8
SEG_ALIGN = V7X_SUBLANES
MOE_ROW_TILE = 512
MOE_FF_CHUNK = 1792


def _params(*sem):
    return pltpu.CompilerParams(dimension_semantics=sem, vmem_limit_bytes=VMEM_LIMIT_BYTES)


def _rms_unit(x):
    return x * lax.rsqrt(jnp.mean(x * x, axis=-1, keepdims=True) + EPS)


def _rms(x, w):
    return _rms_unit(x) * w


def _silu(x):
    return x * jax.nn.sigmoid(x)


def _dot(a, b):
    return jnp.dot(a, b, preferred_element_type=F32)


def _dot_nt(a, b):
    return lax.dot_general(a, b, (((1,), (1,)), ((), ())), preferred_element_type=F32)


def _row_tile(n_rows, want):
    t = min(n_rows, want)
    assert n_rows % t == 0 and (t % V7X_SUBLANES == 0 or t == n_rows)
    return t


def _round_up(v, m):
    return (v + m - 1) // m * m


def _const_spec(shape):
    return pl.BlockSpec(shape, lambda *_: (0,) * len(shape), pipeline_mode=pl.Buffered(1))


def _pw1_glu_kernel(x_ref, nw_ref, w_ref, b_ref, g_ref):
    d = x_ref.shape[1]
    xn = _rms(x_ref[...], nw_ref[...]).astype(BF16)
    u = _dot(xn, w_ref[...]) + b_ref[...]
    g_ref[...] = u[:, :d] * jax.nn.sigmoid(u[:, d:])


def pw1_glu(x, nw, w, b, tm):
    n, d = x.shape
    return pl.pallas_call(
        _pw1_glu_kernel,
        grid=(n // tm,),
        in_specs=[pl.BlockSpec((tm, d), lambda i: (i, 0)),
                  _const_spec((1, d)), _const_spec((d, 2 * d)), _const_spec((1, 2 * d))],
        out_specs=pl.BlockSpec((tm, d), lambda i: (i, 0)),
        out_shape=jax.ShapeDtypeStruct((n, d), F32),
        compiler_params=_params("parallel"),
        name="pw1_glu",
    )(x, nw, w, b)


def _conv_tail(c, x, lng, lnb, w2, b2):
    mu = jnp.mean(c, axis=-1, keepdims=True)
    cc = c - mu
    y = cc * lax.rsqrt(jnp.mean(cc * cc, axis=-1, keepdims=True) + EPS)
    y = _silu(y * lng + lnb)
    return x + _dot(y.astype(BF16), w2) + b2


def _conv_prompt_kernel(g_ref, halo_ref, x_ref, wdw_ref, bdw_ref, lng_ref, lnb_ref, w2_ref, b2_ref,
                        o_ref, win_ref, c_ref, *, tiles_per_seq):
    tm, d = g_ref.shape
    first = (pl.program_id(0) % tiles_per_seq) == 0
    win_ref[0:HALO_ROWS, :] = jnp.where(first, 0.0, halo_ref[...])
    win_ref[HALO_ROWS:HALO_ROWS + tm, :] = g_ref[...]
    lead = HALO_ROWS - (CONV_WIDTH - 1)
    span = HALO_ROWS + CONV_ROW_CHUNK

    def chunk(ci, carry):
        r0 = pl.multiple_of(ci * CONV_ROW_CHUNK, CONV_ROW_CHUNK)
        w = win_ref[pl.ds(r0, span), :]
        acc = jnp.broadcast_to(bdw_ref[...], (CONV_ROW_CHUNK, d))
        for phase in range(V7X_SUBLANES):
            ws = w if phase == 0 else pltpu.roll(w, span - phase, axis=0)
            for off in range(phase, lead + CONV_WIDTH, V7X_SUBLANES):
                j = off - lead
                if j >= 0:
                    base = off - phase
                    acc = acc + wdw_ref[j:j + 1, :] * ws[base:base + CONV_ROW_CHUNK, :]
        c_ref[pl.ds(r0, CONV_ROW_CHUNK), :] = acc
        return carry

    lax.fori_loop(0, tm // CONV_ROW_CHUNK, chunk, 0)
    o_ref[...] = _conv_tail(c_ref[...], x_ref[...], lng_ref[...], lnb_ref[...], w2_ref[...], b2_ref[...])


def conv_prompt(g, x, seq_len, wdw, bdw, lng, lnb, w2, b2, tm):
    n, d = g.shape
    tiles_per_seq = seq_len // tm
    halo_per_tile = tm // HALO_ROWS
    return pl.pallas_call(
        functools.partial(_conv_prompt_kernel, tiles_per_seq=tiles_per_seq),
        grid=(n // tm,),
        in_specs=[pl.BlockSpec((tm, d), lambda i: (i, 0)),
                  pl.BlockSpec((HALO_ROWS, d), lambda i: (jnp.maximum(i * halo_per_tile - 1, 0), 0)),
                  pl.BlockSpec((tm, d), lambda i: (i, 0)),
                  _const_spec((TAP_ROWS, d)), _const_spec((1, d)), _const_spec((1, d)), _const_spec((1, d)),
                  _const_spec((d, d)), _const_spec((1, d))],
        out_specs=pl.BlockSpec((tm, d), lambda i: (i, 0)),
        out_shape=jax.ShapeDtypeStruct((n, d), F32),
        scratch_shapes=[pltpu.VMEM((HALO_ROWS + tm, d), F32), pltpu.VMEM((tm, d), F32)],
        compiler_params=_params("parallel"),
        name="conv_prompt",
    )(g, g, x, wdw, bdw, lng, lnb, w2, b2)


def _conv_sample_kernel(full_ref, x_ref, wdw_ref, bdw_ref, lng_ref, lnb_ref, w2_ref, b2_ref, o_ref, *, n_t):
    nb, d = full_ref.shape[1], full_ref.shape[2]
    for t in range(n_t):
        acc = jnp.broadcast_to(bdw_ref[...], (nb, d))
        for j in range(CONV_WIDTH):
            acc = acc + wdw_ref[j:j + 1, :] * full_ref[t + j]
        rows = slice(t * nb, (t + 1) * nb)
        o_ref[rows, :] = _conv_tail(acc, x_ref[rows, :], lng_ref[...], lnb_ref[...], w2_ref[...], b2_ref[...])


def conv_sample(full_tm, x, wdw, bdw, lng, lnb, w2, b2):
    rows_full, nb, d = full_tm.shape
    n_t = rows_full - (CONV_WIDTH - 1)
    n = n_t * nb
    return pl.pallas_call(
        functools.partial(_conv_sample_kernel, n_t=n_t),
        grid=(1,),
        in_specs=[_const_spec((rows_full, nb, d)), _const_spec((n, d)),
                  _const_spec((TAP_ROWS, d)), _const_spec((1, d)), _const_spec((1, d)), _const_spec((1, d)),
                  _const_spec((d, d)), _const_spec((1, d))],
        out_specs=pl.BlockSpec((n, d), lambda i: (0, 0)),
        out_shape=jax.ShapeDtypeStruct((n, d), F32),
        compiler_params=_params("arbitrary"),
        name="conv_sample",
    )(full_tm, x, wdw, bdw, lng, lnb, w2, b2)


def _ffn_kernel(x_ref, nw_ref, wg_ref, wu_ref, wd_ref, o_ref, h_ref, *, ff_chunk):
    ff = wg_ref.shape[1]
    x = x_ref[...]
    xn = _rms(x, nw_ref[...]).astype(BF16)
    for c in range(ff // ff_chunk):
        cols = slice(c * ff_chunk, (c + 1) * ff_chunk)
        h = _silu(_dot(xn, wg_ref[:, cols])) * _dot(xn, wu_ref[:, cols])
        h_ref[:, cols] = h.astype(BF16)
    o_ref[...] = x + _dot(h_ref[...], wd_ref[...])


def _ff_chunk(ff):
    for c in (512, 256, 128):
        if ff % c == 0:
            return c
    return ff


def ffn(x, nw, wg, wu, wd, tm):
    n, d = x.shape
    ff = wg.shape[1]
    return pl.pallas_call(
        functools.partial(_ffn_kernel, ff_chunk=_ff_chunk(ff)),
        grid=(n // tm,),
        in_specs=[pl.BlockSpec((tm, d), lambda i: (i, 0)), _const_spec((1, d)),
                  _const_spec((d, ff)), _const_spec((d, ff)), _const_spec((ff, d))],
        out_specs=pl.BlockSpec((tm, d), lambda i: (i, 0)),
        out_shape=jax.ShapeDtypeStruct((n, d), F32),
        scratch_shapes=[pltpu.VMEM((tm, ff), BF16)],
        compiler_params=_params("parallel"),
        name="ffn_dense",
    )(x, nw, wg, wu, wd)


def _kvq_kernel(x_ref, nkv_ref, nq_ref, wkv_ref, wq_ref, cos_ref, sin_ref, *refs, dils):
    q_refs, kv_refs, slab = refs[0:N_GROUPS], refs[N_GROUPS:2 * N_GROUPS], refs[2 * N_GROUPS]
    tm = x_ref.shape[0]
    xu = _rms_unit(x_ref[...])
    ykv = _dot((xu * nkv_ref[...]).astype(BF16), wkv_ref[...])
    yq = _dot((xu * nq_ref[...]).astype(BF16), wq_ref[...])
    cos_t, sin_t = cos_ref[...], sin_ref[...]
    lane = lax.broadcasted_iota(jnp.int32, cos_t.shape, 1)
    low_half = (lane % HEAD_DIM) < (ROT_DIM // 2)

    def rope(yb):
        partner = jnp.where(low_half,
                            pltpu.roll(yb, V7X_LANES - ROT_DIM // 2, axis=1),
                            pltpu.roll(yb, ROT_DIM // 2, axis=1))
        return yb * cos_t + partner * sin_t

    slab_id = 0
    for g, dil in enumerate(dils):
        n = tm // dil
        for half in range(GROUP_COLS // V7X_LANES):
            c0 = g * GROUP_COLS + half * V7X_LANES
            src = slice(c0, c0 + V7X_LANES)
            vsrc = slice(N_ATTN + c0, N_ATTN + c0 + V7X_LANES)
            pieces = ((q_refs[g], half * V7X_LANES, rope(yq[:, src])),
                      (kv_refs[g], half * V7X_LANES, rope(ykv[:, src])),
                      (kv_refs[g], GROUP_COLS + half * V7X_LANES, ykv[:, vsrc]))
            for out_ref, col, val in pieces:
                cols = slice(col, col + V7X_LANES)
                if dil == 1:
                    out_ref[0, :, cols] = val.astype(out_ref.dtype)
                else:
                    slab[slab_id] = val
                    for r in range(dil):
                        out_ref[r, :, cols] = slab[slab_id, pl.ds(r, n, stride=dil), :].astype(out_ref.dtype)
                    slab_id += 1


def kvq_proj(x, batch, seq_len, nkv, nq, wkv, wq, cos_t, sin_t, dils, tm):
    n, d = x.shape
    tps = seq_len // tm
    assert all(tm % dil == 0 and (tm // dil) % BF16_SUBLANES == 0 for dil in dils)
    n_slabs = max(1, 3 * (GROUP_COLS // V7X_LANES) * sum(dil > 1 for dil in dils))
    out_specs, out_shape = [], []
    for cols, dt in ((GROUP_COLS, BF16), (2 * GROUP_COLS, F32)):
        for dil in dils:
            out_specs.append(pl.BlockSpec((None, dil, tm // dil, cols), lambda b, i: (b, 0, i, 0)))
            out_shape.append(jax.ShapeDtypeStruct((batch, dil, seq_len // dil, cols), dt))
    return pl.pallas_call(
        functools.partial(_kvq_kernel, dils=dils),
        grid=(batch, tps),
        in_specs=[pl.BlockSpec((tm, d), lambda b, i: (b * tps + i, 0)), _const_spec((1, d)), _const_spec((1, d)),
                  _const_spec(wkv.shape), _const_spec(wq.shape),
                  pl.BlockSpec((tm, V7X_LANES), lambda b, i: (i, 0)),
                  pl.BlockSpec((tm, V7X_LANES), lambda b, i: (i, 0))],
        out_specs=out_specs,
        out_shape=out_shape,
        scratch_shapes=[pltpu.VMEM((n_slabs, tm, V7X_LANES), F32)],
        compiler_params=_params("parallel", "parallel"),
        name="kvq_proj",
    )(x, nkv, nq, wkv, wq, cos_t, sin_t)


def rope_tables(pos):
    half = ROT_DIM // 2
    inv_freq = ROPE_THETA ** (-jnp.arange(half, dtype=F32) / half)
    ang = pos.astype(F32)[:, None] * inv_freq[None, :]
    cos, sin = jnp.cos(ang), jnp.sin(ang)
    ones, zeros = jnp.ones_like(cos), jnp.zeros_like(sin)
    pad = (HEAD_DIM - ROT_DIM) // half
    cos_h = jnp.concatenate([cos, cos] + [ones] * pad, axis=1)
    sin_h = jnp.concatenate([-sin, sin] + [zeros] * pad, axis=1)
    reps = V7X_LANES // HEAD_DIM
    return jnp.tile(cos_h, (1, reps)), jnp.tile(sin_h, (1, reps))


def _head_masks(shape):
    lane = lax.broadcasted_iota(jnp.int32, shape, 1)
    return [(lane // HEAD_DIM) == h for h in range(HEADS_PER_GROUP)]


def _attn_prompt_kernel(q_ref, kvc_ref, kvp_ref, o_ref, l_ref, *, n_keys):
    i = pl.program_id(2)
    tq = n_keys
    n_sub = q_ref.shape[0] // tq
    kv = jnp.concatenate([kvp_ref[...], kvc_ref[...]], axis=0)
    k = kv[:, 0:GROUP_COLS].astype(BF16)
    v = kv[:, GROUP_COLS:2 * GROUP_COLS].astype(BF16)
    hm_q = _head_masks((tq, GROUP_COLS))
    hm_v = _head_masks((2 * tq, GROUP_COLS))
    qi = lax.broadcasted_iota(jnp.int32, (tq, 2 * tq), 0)
    kj = lax.broadcasted_iota(jnp.int32, (tq, 2 * tq), 1)
    rel = qi + tq - kj
    band = (rel >= 0) & (rel <= n_keys)
    bias = jnp.where(band, 0.0, -jnp.inf)
    bias_start = jnp.where(band & (kj >= tq), 0.0, -jnp.inf)
    bias_all = jnp.concatenate([bias] * HEADS_PER_GROUP, axis=0)
    bias_first = jnp.where(i == 0, jnp.concatenate([bias_start] * HEADS_PER_GROUP, axis=0), bias_all)
    for jb in range(n_sub):
        q = q_ref[jb * tq:(jb + 1) * tq, :]
        kb = k[jb * tq:(jb + 2) * tq]
        vb = v[jb * tq:(jb + 2) * tq]
        qs = jnp.concatenate([jnp.where(m, q, jnp.zeros_like(q)) for m in hm_q], axis=0)
        s = _dot_nt(qs, kb) * (HEAD_DIM ** -0.5) + (bias_first if jb == 0 else bias_all)
        m = jnp.max(s, axis=-1, keepdims=True)
        p = jnp.exp(s - m)
        l = jnp.sum(p, axis=-1, keepdims=True)
        pn = (p / l).astype(BF16)
        lse = m + jnp.log(l)
        o = jnp.zeros((tq, GROUP_COLS), F32)
        lmap = jnp.zeros((tq, GROUP_COLS), F32)
        for h in range(HEADS_PER_GROUP):
            rows = slice(h * tq, (h + 1) * tq)
            o = o + _dot(pn[rows], jnp.where(hm_v[h], vb, jnp.zeros_like(vb)))
            lmap = jnp.where(hm_q[h], lse[rows], lmap)
        o_ref[jb * tq:(jb + 1) * tq, :] = o
        l_ref[jb * tq:(jb + 1) * tq, :] = lmap


def attn_prompt(q, kv, group):
    window, dil = GROUPS[group]
    n_keys = window // dil
    batch, _, L, _ = q.shape
    tqb = _row_tile(L, ATTN_Q_ROWS)
    assert tqb % n_keys == 0
    per = tqb // n_keys
    cur = lambda b, r, i: (b, r, i, 0)
    prev = lambda b, r, i: (b, r, jnp.maximum(i * per - 1, 0), 0)
    return pl.pallas_call(
        functools.partial(_attn_prompt_kernel, n_keys=n_keys),
        grid=(batch, dil, L // tqb),
        in_specs=[pl.BlockSpec((None, None, tqb, GROUP_COLS), cur),
                  pl.BlockSpec((None, None, tqb, 2 * GROUP_COLS), cur),
                  pl.BlockSpec((None, None, n_keys, 2 * GROUP_COLS), prev)],
        out_specs=[pl.BlockSpec((None, None, tqb, GROUP_COLS), cur)] * 2,
        out_shape=[jax.ShapeDtypeStruct((batch, dil, L, GROUP_COLS), F32)] * 2,
        compiler_params=_params("parallel", "parallel", "parallel"),
        name=f"attn_prompt_g{group}",
    )(q, kv, kv)


def _attn_sample_kernel(*refs, n_new):
    q_refs, kvn_refs, c_refs = refs[0:3], refs[3:6], refs[6:9]
    o_refs, l_refs, k_all, v_all = refs[9:12], refs[12:15], refs[15], refs[16]
    nq = q_refs[0].shape[0]
    for g in range(N_GROUPS):
        window, dil = GROUPS[g]
        c_ref = c_refs[g]
        buf = c_ref.shape[0]
        n_all = buf + V7X_LANES
        k_all[0:buf, :] = c_ref[:, 0:GROUP_COLS]
        v_all[0:buf, :] = c_ref[:, GROUP_COLS:2 * GROUP_COLS]
        k_all[buf:n_all, :] = jnp.zeros((V7X_LANES, GROUP_COLS), F32)
        v_all[buf:n_all, :] = jnp.zeros((V7X_LANES, GROUP_COLS), F32)
        k_all[buf:buf + n_new, :] = kvn_refs[g][:, 0:GROUP_COLS]
        v_all[buf:buf + n_new, :] = kvn_refs[g][:, GROUP_COLS:2 * GROUP_COLS]
        k = k_all[0:n_all, :].astype(BF16)
        v = v_all[0:n_all, :].astype(BF16)
        q = q_refs[g][...]
        t = lax.broadcasted_iota(jnp.int32, (nq, n_all), 0)
        key = lax.broadcasted_iota(jnp.int32, (nq, n_all), 1)
        rel = buf + t - key
        ok = (rel >= 0) & ((rel & (dil - 1)) == 0) & (rel <= window) & (key < buf + n_new)
        hm_q = _head_masks(q.shape)
        hm_v = _head_masks(v.shape)
        o = jnp.zeros((nq, GROUP_COLS), F32)
        lmap = jnp.zeros((nq, GROUP_COLS), F32)
        for h in range(HEADS_PER_GROUP):
            s = _dot_nt(jnp.where(hm_q[h], q, jnp.zeros_like(q)), k) * (HEAD_DIM ** -0.5)
            s = jnp.where(ok, s, -jnp.inf)
            m = jnp.max(s, axis=-1, keepdims=True)
            p = jnp.exp(s - m)
            l = jnp.sum(p, axis=-1, keepdims=True)
            o = o + _dot((p / l).astype(BF16), jnp.where(hm_v[h], v, jnp.zeros_like(v)))
            lmap = jnp.where(hm_q[h], m + jnp.log(l), lmap)
        o_refs[g][...] = o
        l_refs[g][...] = lmap


def attn_sample(qs, kv_news, caches, n_new):
    nb, nq, _ = qs[0].shape
    max_buf = max(c.shape[1] for c in caches)
    per_b = lambda rows, cols: pl.BlockSpec((None, rows, cols), lambda b: (b, 0, 0))
    in_specs = ([per_b(nq, GROUP_COLS)] * N_GROUPS + [per_b(n_new, 2 * GROUP_COLS)] * N_GROUPS
                + [per_b(c.shape[1], 2 * GROUP_COLS) for c in caches])
    outs = pl.pallas_call(
        functools.partial(_attn_sample_kernel, n_new=n_new),
        grid=(nb,),
        in_specs=in_specs,
        out_specs=[per_b(nq, GROUP_COLS)] * (2 * N_GROUPS),
        out_shape=[jax.ShapeDtypeStruct((nb, nq, GROUP_COLS), F32)] * (2 * N_GROUPS),
        scratch_shapes=[pltpu.VMEM((max_buf + V7X_LANES, GROUP_COLS), F32)] * 2,
        compiler_params=_params("parallel"),
        name="attn_sample",
    )(*qs, *kv_news, *caches)
    return outs[0:N_GROUPS], outs[N_GROUPS:2 * N_GROUPS]


def _combine_wo_kernel(*refs, dils):
    o_refs, l_refs = refs[0:N_GROUPS], refs[N_GROUPS:2 * N_GROUPS]
    x_ref, w_ref, out_ref, slab = refs[2 * N_GROUPS:2 * N_GROUPS + 4]
    tm = x_ref.shape[0]
    slab_id = 0

    def in_position_order(ref, dil):
        nonlocal slab_id
        halves = []
        for half in range(GROUP_COLS // V7X_LANES):
            cols = slice(half * V7X_LANES, (half + 1) * V7X_LANES)
            if dil == 1:
                halves.append(ref[0, :, cols])
            else:
                for r in range(dil):
                    slab[slab_id, pl.ds(r, tm // dil, stride=dil), :] = ref[r, :, cols]
                halves.append(slab[slab_id])
                slab_id += 1
        return jnp.concatenate(halves, axis=1)

    ls = [in_position_order(l_refs[g], dils[g]) for g in range(N_GROUPS)]
    mx = jnp.maximum(jnp.maximum(ls[0], ls[1]), ls[2])
    es = [jnp.exp(l - mx) for l in ls]
    tot = es[0] + es[1] + es[2]
    acc = x_ref[...]
    for g in range(N_GROUPS):
        og = (in_position_order(o_refs[g], dils[g]) * (es[g] / tot)).astype(BF16)
        acc = acc + _dot(og, w_ref[g * GROUP_COLS:(g + 1) * GROUP_COLS, :])
    out_ref[...] = acc


def combine_wo(os_, ls, x, batch, seq_len, w_o, dils, tm):
    n, d = x.shape
    tps = seq_len // tm
    gspecs = [pl.BlockSpec((None, dil, tm // dil, GROUP_COLS), lambda b, i: (b, 0, i, 0)) for dil in dils]
    n_slabs = max(1, 2 * (GROUP_COLS // V7X_LANES) * sum(dil > 1 for dil in dils))
    return pl.pallas_call(
        functools.partial(_combine_wo_kernel, dils=dils),
        grid=(batch, tps),
        in_specs=gspecs + gspecs + [pl.BlockSpec((tm, d), lambda b, i: (b * tps + i, 0)), _const_spec(w_o.shape)],
        out_specs=pl.BlockSpec((tm, d), lambda b, i: (b * tps + i, 0)),
        out_shape=jax.ShapeDtypeStruct((n, d), F32),
        scratch_shapes=[pltpu.VMEM((n_slabs, tm, V7X_LANES), F32)],
        compiler_params=_params("parallel", "parallel"),
        name="combine_wo",
    )(*os_, *ls, x, w_o)


def _router_kernel(x_ref, nw_ref, wr_ref, xn_ref, meta_ref, metat_ref, cnt_ref, *, n_experts):
    xn = _rms(x_ref[...], nw_ref[...])
    xn_ref[...] = xn.astype(BF16)
    xh = xn.astype(BF16)
    xl = (xn - xh.astype(F32)).astype(BF16)
    wr = wr_ref[...]
    wh = wr.astype(BF16)
    wl = (wr - wh.astype(F32)).astype(BF16)
    logits = _dot(xh, wh) + (_dot(xl, wh) + _dot(xh, wl))
    lane = lax.broadcasted_iota(jnp.int32, logits.shape, 1)
    logits = jnp.where(lane < n_experts, logits, -jnp.inf)
    v1 = jnp.max(logits, axis=-1, keepdims=True)
    i1 = jnp.min(jnp.where(logits == v1, lane, V7X_LANES), axis=-1, keepdims=True)
    rest = jnp.where(lane == i1, -jnp.inf, logits)
    v2 = jnp.max(rest, axis=-1, keepdims=True)
    i2 = jnp.min(jnp.where(rest == v2, lane, V7X_LANES), axis=-1, keepdims=True)
    e2 = jnp.exp(v2 - v1)
    den = 1.0 + e2
    g1, g2 = 1.0 / den, e2 / den
    onehot = ((lane == i1) | (lane == i2)).astype(BF16)
    tm = onehot.shape[0]
    tri = (lax.broadcasted_iota(jnp.int32, (tm, tm), 0) >= lax.broadcasted_iota(jnp.int32, (tm, tm), 1)).astype(BF16)
    csum = _dot(tri, onehot)
    r1 = jnp.sum(jnp.where(lane == i1, csum, 0.0), axis=-1, keepdims=True) - 1.0
    r2 = jnp.sum(jnp.where(lane == i2, csum, 0.0), axis=-1, keepdims=True) - 1.0
    cnt_ref[...] = csum[tm - 1:tm, :].astype(jnp.int32)
    cols = (i1.astype(F32), i2.astype(F32), r1, r2, g1, g2)
    meta = jnp.zeros(logits.shape, F32)
    for ci, col in enumerate(cols):
        meta = jnp.where(lane == ci, col, meta)
    meta_ref[...] = meta[:, :META_COLS]
    metat_ref[...] = meta.T[:META_COLS, :]


def router(x, nw, w_router, tm):
    n, d = x.shape
    n_experts = w_router.shape[1]
    wr = jnp.pad(w_router, ((0, 0), (0, V7X_LANES - n_experts)))
    return pl.pallas_call(
        functools.partial(_router_kernel, n_experts=n_experts),
        grid=(n // tm,),
        in_specs=[pl.BlockSpec((tm, d), lambda i: (i, 0)), _const_spec((1, d)), _const_spec((d, V7X_LANES))],
        out_specs=[pl.BlockSpec((tm, d), lambda i: (i, 0)),
                   pl.BlockSpec((tm, META_COLS), lambda i: (i, 0)),
                   pl.BlockSpec((None, META_COLS, tm), lambda i: (i, 0, 0)),
                   pl.BlockSpec((None, 1, V7X_LANES), lambda i: (i, 0, 0))],
        out_shape=[jax.ShapeDtypeStruct((n, d), BF16),
                   jax.ShapeDtypeStruct((n, META_COLS), F32),
                   jax.ShapeDtypeStruct((n // tm, META_COLS, tm), F32),
                   jax.ShapeDtypeStruct((n // tm, 1, V7X_LANES), jnp.int32)],
        compiler_params=_params("parallel"),
        name="router",
    )(x, nw, wr)


def _segment_bits(tm):
    bits, b = [], SEG_ALIGN
    while b <= tm:
        bits.append(b)
        b *= 2
    return bits[::-1]


def _compact_rows(tm, n_experts):
    worst = TOP_K * tm + n_experts * (SEG_ALIGN - 1)
    return _round_up(worst, BF16_SUBLANES)


def _segment_copies(seg_sm, base_sm, tile, n_experts, tm, hbm_ref, vmem_ref, sems, to_hbm):
    out = []
    local = 0
    for e in range(n_experts):
        n = seg_sm[tile * n_experts + e]
        base = base_sm[tile * n_experts + e]
        for b, bit in enumerate(_segment_bits(tm)):
            done = n & ~(2 * bit - 1)
            src = vmem_ref.at[pl.ds(pl.multiple_of(local + done, SEG_ALIGN), bit)]
            dst = hbm_ref.at[pl.ds(pl.multiple_of(base + done, SEG_ALIGN), bit)]
            if not to_hbm:
                src, dst = dst, src
            out.append(((n & bit) != 0, pltpu.make_async_copy(src, dst, sems.at[e, b])))
        local = local + n
    return out, local


def _token_dest(e_k, r_k, seg_sm, tile, n_experts):
    dest = r_k
    local = 0
    for e in range(n_experts):
        dest = dest + jnp.where(e_k == e, local, 0)
        local = local + seg_sm[tile * n_experts + e]
    return dest


def _start_all(copies):
    for cond, cp in copies:
        pl.when(cond)(cp.start)


def _wait_all(copies):
    for cond, cp in copies:
        pl.when(cond)(cp.wait)


def _zero_copies(zn_sm, zbase_sm, n_spans, zero_rows, xs_hbm, zero_ref, zero_sems):
    out = []
    for k in range(n_spans):
        n, base = zn_sm[k], zbase_sm[k]
        for b, bit in enumerate(_segment_bits(zero_rows)):
            done = n & ~(2 * bit - 1)
            dst = xs_hbm.at[pl.ds(pl.multiple_of(base + done, SEG_ALIGN), bit)]
            out.append(((n & bit) != 0, pltpu.make_async_copy(zero_ref.at[pl.ds(0, bit)], dst, zero_sems.at[k, b])))
    return out


def _dispatch_kernel(seg_sm, base_sm, zn_sm, zbase_sm, tail_sm, xn_ref, metat_ref, *refs,
                     n_experts, n_tiles, tile0, n_spans, tmg, first_call):
    xs_hbm, comp_ref, zero_ref, sems, zero_sems, tail_sems = refs[-6:]
    i = pl.program_id(0)
    tile, slot = tile0 + i, i % 2
    tm, d = xn_ref.shape
    rows = comp_ref.shape[1]
    zero_rows = zero_ref.shape[0]

    def copies(t, s):
        return _segment_copies(seg_sm, base_sm, t, n_experts, tm, xs_hbm, comp_ref.at[s], sems.at[s], to_hbm=True)[0]

    if first_call:
        zeros = _zero_copies(zn_sm, zbase_sm, n_spans, zero_rows, xs_hbm, zero_ref, zero_sems)

        @pl.when(i == 0)
        def _():
            zero_ref[...] = jnp.zeros_like(zero_ref)
            _start_all(zeros)

    @pl.when(i < n_tiles)
    def _():
        mt = metat_ref[...]
        as_int = lambda v: v.astype(jnp.int32)
        dest1 = _token_dest(as_int(mt[0:1, :]), as_int(mt[2:3, :]), seg_sm, tile, n_experts)
        dest2 = _token_dest(as_int(mt[1:2, :]), as_int(mt[3:4, :]), seg_sm, tile, n_experts)
        row_id = lax.broadcasted_iota(jnp.int32, (rows, tm), 0)
        p1, p2 = row_id == dest1, row_id == dest2
        comp_ref[slot, :, 0:d] = _dot((p1 | p2).astype(BF16), xn_ref[...])
        gate = jnp.sum(jnp.where(p1, mt[4:5, :], 0.0) + jnp.where(p2, mt[5:6, :], 0.0), axis=-1, keepdims=True)
        comp_ref[slot, :, d:d + V7X_LANES] = jnp.broadcast_to(gate, (rows, V7X_LANES))
        _start_all(copies(tile, slot))

    @pl.when((i > 0) & (i <= n_tiles))
    def _():
        _wait_all(copies(tile - 1, 1 - slot))

    if first_call:
        @pl.when(i == n_tiles)
        def _():
            _wait_all(zeros)

        @pl.when((i > n_tiles) & (i - n_tiles - 1 < tail_sm[1]))
        def _():
            start = tail_sm[0] + (i - n_tiles - 1) * tmg
            parts = [pltpu.make_async_copy(
                zero_ref, xs_hbm.at[pl.ds(pl.multiple_of(start + p * zero_rows, SEG_ALIGN), zero_rows)], tail_sems.at[p])
                for p in range(tmg // zero_rows)]
            for cp in parts:
                cp.start()
            for cp in parts:
                cp.wait()


def dispatch(xn, metat, tables, n_experts, total_rows, tail_tiles, tm, tmg, tile0, xs_prev):
    n, d = xn.shape
    n_tiles = n // tm
    n_spans = tables[2].shape[0]
    rows = _compact_rows(tm, n_experts)
    zero_rows = tmg // 2
    first_call = xs_prev is None
    last_tile = lambda i: jnp.minimum(i, n_tiles - 1)
    in_specs = [pl.BlockSpec((tm, d), lambda i, *_: (last_tile(i), 0)),
                pl.BlockSpec((None, META_COLS, tm), lambda i, *_: (last_tile(i), 0, 0))]
    args = [*tables, xn, metat]
    if not first_call:
        in_specs.append(pl.BlockSpec(memory_space=pl.ANY))
        args.append(xs_prev)
    grid_spec = pltpu.PrefetchScalarGridSpec(
        num_scalar_prefetch=len(tables),
        grid=(n_tiles + 1 + (tail_tiles if first_call else 0),),
        in_specs=in_specs,
        out_specs=pl.BlockSpec(memory_space=pl.ANY),
        scratch_shapes=[pltpu.VMEM((2, rows, d + V7X_LANES), F32),
                        pltpu.VMEM((zero_rows, d + V7X_LANES), F32),
                        pltpu.SemaphoreType.DMA((2, n_experts, len(_segment_bits(tm)))),
                        pltpu.SemaphoreType.DMA((n_spans, len(_segment_bits(zero_rows)))),
                        pltpu.SemaphoreType.DMA((tmg // zero_rows,))],
    )
    return pl.pallas_call(
        functools.partial(_dispatch_kernel, n_experts=n_experts, n_tiles=n_tiles, tile0=tile0, n_spans=n_spans,
                          tmg=tmg, first_call=first_call),
        grid_spec=grid_spec,
        out_shape=jax.ShapeDtypeStruct((total_rows, d + V7X_LANES), F32),
        input_output_aliases={} if first_call else {len(args) - 1: 0},
        compiler_params=_params("arbitrary"),
        name="moe_dispatch",
    )(*args)


def _expert_ffn_kernel(te_sm, nu_sm, xs_ref, wg_ref, wu_ref, wd_ref, ys_ref, xb_ref, acc_ref):
    del te_sm
    j, c = pl.program_id(0), pl.program_id(1)
    last_c = pl.num_programs(1) - 1
    d = xb_ref.shape[1]
    used = j < nu_sm[0]

    @pl.when(used & (c == 0))
    def _():
        xb_ref[...] = xs_ref[:, 0:d].astype(BF16)
        acc_ref[...] = jnp.zeros_like(acc_ref)

    @pl.when(used)
    def _():
        xb = xb_ref[...]
        h = _silu(_dot(xb, wg_ref[...])) * _dot(xb, wu_ref[...])
        acc_ref[...] += _dot(h.astype(BF16), wd_ref[...])

    @pl.when(used & (c == last_c))
    def _():
        ys_ref[...] = xs_ref[:, d:d + 1] * acc_ref[...]

    @pl.when(jnp.logical_not(used) & (c == last_c))
    def _():
        ys_ref[...] = jnp.zeros_like(ys_ref)


def expert_ffn(xs, tile_expert, n_used, wg, wu, wd, tmg):
    total_rows, dx = xs.shape
    n_experts, d, de = wg.shape
    tf = next(c for c in (MOE_FF_CHUNK, MOE_FF_CHUNK // 2, _ff_chunk(de)) if de % c == 0)
    last_used = lambda j, nu: jnp.minimum(j, nu[0] - 1)
    grid_spec = pltpu.PrefetchScalarGridSpec(
        num_scalar_prefetch=2,
        grid=(total_rows // tmg, de // tf),
        in_specs=[pl.BlockSpec((tmg, dx), lambda j, c, te, nu: (last_used(j, nu), 0)),
                  pl.BlockSpec((None, d, tf), lambda j, c, te, nu: (te[j], 0, c)),
                  pl.BlockSpec((None, d, tf), lambda j, c, te, nu: (te[j], 0, c)),
                  pl.BlockSpec((None, tf, d), lambda j, c, te, nu: (te[j], c, 0))],
        out_specs=pl.BlockSpec((tmg, d), lambda j, c, te, nu: (j, 0)),
        scratch_shapes=[pltpu.VMEM((tmg, d), BF16), pltpu.VMEM((tmg, d), F32)],
    )
    return pl.pallas_call(
        _expert_ffn_kernel,
        grid_spec=grid_spec,
        out_shape=jax.ShapeDtypeStruct((total_rows, d), F32),
        compiler_params=_params("arbitrary", "arbitrary"),
        name="moe_expert_ffn",
    )(tile_expert, n_used, xs, wg, wu, wd)


def _combine_kernel(seg_sm, base_sm, x_ref, meta_ref, nf_ref, ys_hbm, y_ref, comp_ref, sems, *, n_experts, tile0):
    i, n_steps = pl.program_id(0), pl.num_programs(0)
    tile, slot = tile0 + i, i % 2
    tm, d = x_ref.shape
    rows = comp_ref.shape[1]

    def copies(t, s):
        return _segment_copies(seg_sm, base_sm, t, n_experts, tm, ys_hbm, comp_ref.at[s], sems.at[s], to_hbm=False)

    @pl.when(i == 0)
    def _():
        _start_all(copies(tile, slot)[0])

    @pl.when(i + 1 < n_steps)
    def _():
        _start_all(copies(tile + 1, 1 - slot)[0])

    meta = meta_ref[...]
    as_int = lambda v: v.astype(jnp.int32)
    dest1 = _token_dest(as_int(meta[:, 0:1]), as_int(meta[:, 2:3]), seg_sm, tile, n_experts)
    dest2 = _token_dest(as_int(meta[:, 1:2]), as_int(meta[:, 3:4]), seg_sm, tile, n_experts)
    col_id = lax.broadcasted_iota(jnp.int32, (tm, rows), 1)
    pick = ((col_id == dest1) | (col_id == dest2)).astype(BF16)
    mine, n_rows = copies(tile, slot)
    _wait_all(mine)
    live = lax.broadcasted_iota(jnp.int32, (rows, 1), 0) < n_rows
    ys = jnp.where(live, comp_ref[slot], 0.0)
    hi = ys.astype(BF16)
    lo = (ys - hi.astype(F32)).astype(BF16)
    y_ref[...] = _rms(x_ref[...] + (_dot(pick, hi) + _dot(pick, lo)), nf_ref[...])


def combine_final(x, meta, nf, ys, seg, base, n_experts, tm, tile0):
    n, d = x.shape
    rows = _compact_rows(tm, n_experts)
    grid_spec = pltpu.PrefetchScalarGridSpec(
        num_scalar_prefetch=2,
        grid=(n // tm,),
        in_specs=[pl.BlockSpec((tm, d), lambda i, *_: (i, 0)),
                  pl.BlockSpec((tm, META_COLS), lambda i, *_: (i, 0)),
                  pl.BlockSpec((1, d), lambda i, *_: (0, 0)),
                  pl.BlockSpec(memory_space=pl.ANY)],
        out_specs=pl.BlockSpec((tm, d), lambda i, *_: (i, 0)),
        scratch_shapes=[pltpu.VMEM((2, rows, d), F32),
                        pltpu.SemaphoreType.DMA((2, n_experts, len(_segment_bits(tm))))],
    )
    return pl.pallas_call(
        functools.partial(_combine_kernel, n_experts=n_experts, tile0=tile0),
        grid_spec=grid_spec,
        out_shape=jax.ShapeDtypeStruct((n, d), F32),
        compiler_params=_params("arbitrary"),
        name="moe_combine",
    )(seg, base, x, meta, nf, ys)


def moe_final(xs, tms, nw, nf, w_router, wg, wu, wd, tmg):
    n_experts = w_router.shape[1]
    routed = [router(x, nw, w_router, tm) for x, tm in zip(xs, tms)]
    n_tiles = [x.shape[0] // tm for x, tm in zip(xs, tms)]
    n_tokens = sum(x.shape[0] for x in xs)
    cnt = jnp.concatenate([r[3][:, 0, :n_experts] for r in routed], axis=0)
    seg = _round_up(cnt, SEG_ALIGN)
    within = jnp.cumsum(seg, axis=0) - seg
    exp_live = jnp.sum(seg, axis=0)
    exp_rows = _round_up(exp_live, tmg)
    exp_end = jnp.cumsum(exp_rows)
    base = (exp_end - exp_rows)[None, :] + within
    total_rows = _round_up(TOP_K * n_tokens + sum(n_tiles) * n_experts * (SEG_ALIGN - 1)
                           + n_experts * (tmg - SEG_ALIGN), tmg)
    n_used = (exp_end[-1] // tmg).reshape(1).astype(jnp.int32)
    tile_start = jnp.arange(total_rows // tmg, dtype=jnp.int32) * tmg
    tile_expert = jnp.minimum(jnp.sum(tile_start[:, None] >= exp_end[None, :], axis=1), n_experts - 1).astype(jnp.int32)
    i32 = lambda a: a.reshape(-1).astype(jnp.int32)
    zero_n = jnp.concatenate([exp_rows - exp_live, seg[n_tiles[0]:].reshape(-1)])
    zero_base = jnp.concatenate([exp_end - exp_rows + exp_live, base[n_tiles[0]:].reshape(-1)])
    tail = jnp.stack([exp_end[-1], (total_rows - exp_end[-1]) // tmg])
    tail_tiles = (total_rows - TOP_K * n_tokens) // tmg
    tables = (i32(seg), i32(base), i32(zero_n), i32(zero_base), i32(tail))
    rows_buf, tile0 = None, 0
    for (xn, _, metat, _), x, tm, nt in zip(routed, xs, tms, n_tiles):
        rows_buf = dispatch(xn, metat, tables, n_experts, total_rows, tail_tiles, tm, tmg, tile0, rows_buf)
        tile0 += nt
    ys = expert_ffn(rows_buf, tile_expert, n_used, wg, wu, wd, tmg)
    outs, tile0 = [], 0
    for (_, meta, _, _), x, tm, nt in zip(routed, xs, tms, n_tiles):
        outs.append(combine_final(x, meta, nf, ys, tables[0], tables[1], n_experts, tm, tile0))
        tile0 += nt
    return outs


def kernel(x_prompt, x_sample, state_conv, cache_kv_g0, cache_kv_g1, cache_kv_g2, norm_mix, norm_ffn, norm_kv, norm_final, w_pw1, b_pw1, w_dw, b_dw, ln_conv_g, ln_conv_b, w_pw2, b_pw2, w_q, w_kv, w_o, w_gate_dense, w_up_dense, w_down_dense, w_router, w_gate_exp, w_up_exp, w_down_exp):
    batch, seq_len, d = x_prompt.shape
    dec_batch, dec_seq, _ = x_sample.shape
    caches = (cache_kv_g0, cache_kv_g1, cache_kv_g2)
    assert norm_mix.shape[0] == 2 and state_conv.shape[0] == 1 and w_q.shape[0] == 1
    assert all(c.shape[1] == w for c, (w, _) in zip(caches, GROUPS))
    assert dec_seq <= SAMPLE_Q_ROWS

    row = lambda v: v.reshape(1, -1).astype(F32)
    bf = lambda w: w.astype(BF16)
    n_p = batch * seq_len
    n_s = dec_batch * dec_seq
    tm_p = _row_tile(seq_len, 512)
    tm_s = n_s
    dils = tuple(dil for _, dil in GROUPS)
    no_dil = (1,) * N_GROUPS

    wdw = jnp.pad(w_dw[0], ((0, TAP_ROWS - CONV_WIDTH), (0, 0)))
    conv_w = (wdw, row(b_dw[0]), row(ln_conv_g[0]), row(ln_conv_b[0]), bf(w_pw2[0]), row(b_pw2[0]))
    w_pw1_b, b_pw1_r = bf(w_pw1[0]), row(b_pw1[0])
    ffn_w = (bf(w_gate_dense[0]), bf(w_up_dense[0]), bf(w_down_dense[0]))
    kvq_w = (row(norm_kv), row(norm_mix[1]), bf(w_kv), bf(w_q[0]))
    w_o_b = bf(w_o[0])
    moe_w = (bf(w_gate_exp[0]), bf(w_up_exp[0]), bf(w_down_exp[0]))

    cos_p, sin_p = rope_tables(jnp.arange(seq_len))
    cos_s, sin_s = rope_tables(PAST_LEN + jnp.arange(n_s) // dec_batch)

    hp = x_prompt.reshape(n_p, d)
    hs = x_sample.transpose(1, 0, 2).reshape(n_s, d)
    g_p = pw1_glu(hp, row(norm_mix[0]), w_pw1_b, b_pw1_r, tm_p)
    g_s = pw1_glu(hs, row(norm_mix[0]), w_pw1_b, b_pw1_r, tm_s)
    hp = conv_prompt(g_p, hp, seq_len, *conv_w, tm_p)
    full_s = jnp.concatenate([state_conv[0].transpose(1, 0, 2), g_s.reshape(dec_seq, dec_batch, d)], axis=0)
    hs = conv_sample(full_s, hs, *conv_w)
    conv_prompt_out = g_p.reshape(batch, seq_len, d)[:, seq_len - (CONV_WIDTH - 1):][None]
    conv_sample_out = full_s[dec_seq:].transpose(1, 0, 2)[None]

    hp = ffn(hp, row(norm_ffn[0]), *ffn_w, tm_p)
    hs = ffn(hs, row(norm_ffn[0]), *ffn_w, tm_s)

    kvq_p = kvq_proj(hp, batch, seq_len, *kvq_w, cos_p, sin_p, dils, tm_p)
    q_p, kv_p = kvq_p[0:N_GROUPS], kvq_p[N_GROUPS:]
    kvq_s = kvq_proj(hs, 1, n_s, *kvq_w, cos_s, sin_s, no_dil, tm_s)
    to_bm = lambda a: a.reshape(dec_seq, dec_batch, -1).transpose(1, 0, 2)
    q_s = [jnp.pad(to_bm(a), ((0, 0), (0, SAMPLE_Q_ROWS - dec_seq), (0, 0))) for a in kvq_s[0:N_GROUPS]]
    kv_s = [to_bm(a) for a in kvq_s[N_GROUPS:]]

    att_p = [attn_prompt(q_p[g], kv_p[g], g) for g in range(N_GROUPS)]
    hp = combine_wo([a[0] for a in att_p], [a[1] for a in att_p], hp, batch, seq_len, w_o_b, dils, tm_p)

    hs = to_bm(hs).reshape(n_s, d)
    caches2 = [c.reshape(dec_batch, c.shape[1], 2 * GROUP_COLS) for c in caches]
    o_s, l_s = attn_sample(q_s, kv_s, caches2, dec_seq)
    as_rows = lambda a: a[:, :dec_seq].reshape(1, 1, n_s, GROUP_COLS)
    hs = combine_wo([as_rows(a) for a in o_s], [as_rows(a) for a in l_s], hs, 1, n_s, w_o_b, no_dil, tm_s)

    y_p, y_s = moe_final([hp, hs], [tm_p, tm_s], row(norm_ffn[1]), row(norm_final), w_router[0], *moe_w,
                         MOE_ROW_TILE)

    kv_out = []
    for g, (window, dil) in enumerate(GROUPS):
        keep = min(window, seq_len)
        tail = kv_p[g][:, :, (seq_len - keep) // dil:, :]
        kv_out.append(tail.transpose(0, 2, 1, 3).reshape(batch, keep, 2, HEADS_PER_GROUP, HEAD_DIM))
        kv_out.append(kv_s[g].reshape(dec_batch, dec_seq, 2, HEADS_PER_GROUP, HEAD_DIM))
    return (y_p.reshape(batch, seq_len, d), y_s.reshape(dec_batch, dec_seq, d),
            conv_prompt_out, conv_sample_out, *kv_out)
```

```python
import functools

import jax
import jax.numpy as jnp
from jax import lax
from jax.experimental import pallas as pl
from jax.experimental.pallas import tpu as pltpu

F32 = jnp.float32
BF16 = jnp.bfloat16

EPS = 1e-5
CONV_WIDTH = 31
HEAD_DIM = 64
ROT_DIM = HEAD_DIM // 4
HEADS_PER_GROUP = 4
GROUP_COLS = HEADS_PER_GROUP * HEAD_DIM
GROUPS = ((128, 1), (512, 4), (2048, 16))
N_GROUPS = len(GROUPS)
N_ATTN = N_GROUPS * GROUP_COLS
ROPE_THETA = 500000.0
TOP_K = 2
PAST_LEN = 16384

V7X_LANES = 128
V7X_SUBLANES = 8
V7X_VMEM_BYTES = 64 * 2**20
VMEM_LIMIT_BYTES = V7X_VMEM_BYTES - 8 * 2**20
BF16_SUBLANES = 2 * V7X_SUBLANES

HALO_ROWS = 32
TAP_ROWS = 32
CONV_ROW_CHUNK = 32
ATTN_Q_ROWS = 512
SAMPLE_Q_ROWS = 8
META_COLS = 8
SEG_ALIGN = V7X_SUBLANES
MOE_ROW_TILE = 512
MOE_FF_CHUNK = 1792


def _params(*sem):
    return pltpu.CompilerParams(dimension_semantics=sem, vmem_limit_bytes=VMEM_LIMIT_BYTES)


def _rms_unit(x):
    return x * lax.rsqrt(jnp.mean(x * x, axis=-1, keepdims=True) + EPS)


def _rms(x, w):
    return _rms_unit(x) * w


def _silu(x):
    return x * jax.nn.sigmoid(x)


def _dot(a, b):
    return jnp.dot(a, b, preferred_element_type=F32)


def _dot_nt(a, b):
    return lax.dot_general(a, b, (((1,), (1,)), ((), ())), preferred_element_type=F32)


def _row_tile(n_rows, want):
    t = min(n_rows, want)
    assert n_rows % t == 0 and (t % V7X_SUBLANES == 0 or t == n_rows)
    return t


def _round_up(v, m):
    return (v + m - 1) // m * m


def _const_spec(shape):
    return pl.BlockSpec(shape, lambda *_: (0,) * len(shape), pipeline_mode=pl.Buffered(1))


def _pw1_glu_kernel(x_ref, nw_ref, w_ref, b_ref, g_ref):
    d = x_ref.shape[1]
    xn = _rms(x_ref[...], nw_ref[...]).astype(BF16)
    u = _dot(xn, w_ref[...]) + b_ref[...]
    g_ref[...] = u[:, :d] * jax.nn.sigmoid(u[:, d:])


def pw1_glu(x, nw, w, b, tm):
    n, d = x.shape
    return pl.pallas_call(
        _pw1_glu_kernel,
        grid=(n // tm,),
        in_specs=[pl.BlockSpec((tm, d), lambda i: (i, 0)),
                  _const_spec((1, d)), _const_spec((d, 2 * d)), _const_spec((1, 2 * d))],
        out_specs=pl.BlockSpec((tm, d), lambda i: (i, 0)),
        out_shape=jax.ShapeDtypeStruct((n, d), F32),
        compiler_params=_params("parallel"),
        name="pw1_glu",
    )(x, nw, w, b)


def _conv_tail(c, x, lng, lnb, w2, b2):
    mu = jnp.mean(c, axis=-1, keepdims=True)
    cc = c - mu
    y = cc * lax.rsqrt(jnp.mean(cc * cc, axis=-1, keepdims=True) + EPS)
    y = _silu(y * lng + lnb)
    return x + _dot(y.astype(BF16), w2) + b2


def _conv_prompt_kernel(g_ref, halo_ref, x_ref, wdw_ref, bdw_ref, lng_ref, lnb_ref, w2_ref, b2_ref,
                        o_ref, win_ref, c_ref, *, tiles_per_seq):
    tm, d = g_ref.shape
    first = (pl.program_id(0) % tiles_per_seq) == 0
    win_ref[0:HALO_ROWS, :] = jnp.where(first, 0.0, halo_ref[...])
    win_ref[HALO_ROWS:HALO_ROWS + tm, :] = g_ref[...]
    lead = HALO_ROWS - (CONV_WIDTH - 1)
    span = HALO_ROWS + CONV_ROW_CHUNK

    def chunk(ci, carry):
        r0 = pl.multiple_of(ci * CONV_ROW_CHUNK, CONV_ROW_CHUNK)
        w = win_ref[pl.ds(r0, span), :]
        acc = None
        for phase in range(V7X_SUBLANES):
            ws = w if phase == 0 else pltpu.roll(w, span - phase, axis=0)
            for off in range(phase, lead + CONV_WIDTH, V7X_SUBLANES):
                j = off - lead
                if j >= 0:
                    base = off - phase
                    rows = ws[base:base + CONV_ROW_CHUNK, :].reshape(CONV_ROW_CHUNK // V7X_SUBLANES, V7X_SUBLANES, d)
                    term = wdw_ref[j] * rows
                    acc = term if acc is None else acc + term
        c_ref[pl.ds(r0, CONV_ROW_CHUNK), :] = acc.reshape(CONV_ROW_CHUNK, d) + bdw_ref[...]
        return carry

    lax.fori_loop(0, tm // CONV_ROW_CHUNK, chunk, 0)
    o_ref[...] = _conv_tail(c_ref[...], x_ref[...], lng_ref[...], lnb_ref[...], w2_ref[...], b2_ref[...])


def conv_prompt(g, x, seq_len, wdw, bdw, lng, lnb, w2, b2, tm):
    n, d = g.shape
    tiles_per_seq = seq_len // tm
    halo_per_tile = tm // HALO_ROWS
    return pl.pallas_call(
        functools.partial(_conv_prompt_kernel, tiles_per_seq=tiles_per_seq),
        grid=(n // tm,),
        in_specs=[pl.BlockSpec((tm, d), lambda i: (i, 0)),
                  pl.BlockSpec((HALO_ROWS, d), lambda i: (jnp.maximum(i * halo_per_tile - 1, 0), 0)),
                  pl.BlockSpec((tm, d), lambda i: (i, 0)),
                  _const_spec(wdw.shape), _const_spec((1, d)), _const_spec((1, d)), _const_spec((1, d)),
                  _const_spec((d, d)), _const_spec((1, d))],
        out_specs=pl.BlockSpec((tm, d), lambda i: (i, 0)),
        out_shape=jax.ShapeDtypeStruct((n, d), F32),
        scratch_shapes=[pltpu.VMEM((HALO_ROWS + tm, d), F32), pltpu.VMEM((tm, d), F32)],
        compiler_params=_params("parallel"),
        name="conv_prompt",
    )(g, g, x, wdw, bdw, lng, lnb, w2, b2)


def _conv_sample_kernel(full_ref, x_ref, wdw_ref, bdw_ref, lng_ref, lnb_ref, w2_ref, b2_ref, o_ref, *, n_t):
    nb, d = full_ref.shape[1], full_ref.shape[2]
    for t in range(n_t):
        acc = jnp.broadcast_to(bdw_ref[...], (nb, d))
        for j in range(CONV_WIDTH):
            acc = acc + wdw_ref[j:j + 1, :] * full_ref[t + j]
        rows = slice(t * nb, (t + 1) * nb)
        o_ref[rows, :] = _conv_tail(acc, x_ref[rows, :], lng_ref[...], lnb_ref[...], w2_ref[...], b2_ref[...])


def conv_sample(full_tm, x, wdw, bdw, lng, lnb, w2, b2):
    rows_full, nb, d = full_tm.shape
    n_t = rows_full - (CONV_WIDTH - 1)
    n = n_t * nb
    return pl.pallas_call(
        functools.partial(_conv_sample_kernel, n_t=n_t),
        grid=(1,),
        in_specs=[_const_spec((rows_full, nb, d)), _const_spec((n, d)),
                  _const_spec((TAP_ROWS, d)), _const_spec((1, d)), _const_spec((1, d)), _const_spec((1, d)),
                  _const_spec((d, d)), _const_spec((1, d))],
        out_specs=pl.BlockSpec((n, d), lambda i: (0, 0)),
        out_shape=jax.ShapeDtypeStruct((n, d), F32),
        compiler_params=_params("arbitrary"),
        name="conv_sample",
    )(full_tm, x, wdw, bdw, lng, lnb, w2, b2)


def _ffn_kernel(x_ref, nw_ref, wg_ref, wu_ref, wd_ref, o_ref, h_ref, *, ff_chunk):
    ff = wg_ref.shape[1]
    x = x_ref[...]
    xn = _rms(x, nw_ref[...]).astype(BF16)
    for c in range(ff // ff_chunk):
        cols = slice(c * ff_chunk, (c + 1) * ff_chunk)
        h = _silu(_dot(xn, wg_ref[:, cols])) * _dot(xn, wu_ref[:, cols])
        h_ref[:, cols] = h.astype(BF16)
    o_ref[...] = x + _dot(h_ref[...], wd_ref[...])


def _ff_chunk(ff):
    for c in (512, 256, 128):
        if ff % c == 0:
            return c
    return ff


def ffn(x, nw, wg, wu, wd, tm):
    n, d = x.shape
    ff = wg.shape[1]
    return pl.pallas_call(
        functools.partial(_ffn_kernel, ff_chunk=_ff_chunk(ff)),
        grid=(n // tm,),
        in_specs=[pl.BlockSpec((tm, d), lambda i: (i, 0)), _const_spec((1, d)),
                  _const_spec((d, ff)), _const_spec((d, ff)), _const_spec((ff, d))],
        out_specs=pl.BlockSpec((tm, d), lambda i: (i, 0)),
        out_shape=jax.ShapeDtypeStruct((n, d), F32),
        scratch_shapes=[pltpu.VMEM((tm, ff), BF16)],
        compiler_params=_params("parallel"),
        name="ffn_dense",
    )(x, nw, wg, wu, wd)


def _kvq_kernel(x_ref, nkv_ref, nq_ref, wkv_ref, wq_ref, cos_ref, sin_ref, *refs, dils):
    q_refs, kv_refs, slab = refs[0:N_GROUPS], refs[N_GROUPS:2 * N_GROUPS], refs[2 * N_GROUPS]
    tm = x_ref.shape[0]
    xu = _rms_unit(x_ref[...])
    ykv = _dot((xu * nkv_ref[...]).astype(BF16), wkv_ref[...])
    yq = _dot((xu * nq_ref[...]).astype(BF16), wq_ref[...])
    cos_t, sin_t = cos_ref[...], sin_ref[...]
    lane = lax.broadcasted_iota(jnp.int32, cos_t.shape, 1)
    low_half = (lane % HEAD_DIM) < (ROT_DIM // 2)

    def rope(yb):
        partner = jnp.where(low_half,
                            pltpu.roll(yb, V7X_LANES - ROT_DIM // 2, axis=1),
                            pltpu.roll(yb, ROT_DIM // 2, axis=1))
        return yb * cos_t + partner * sin_t

    slab_id = 0
    for g, dil in enumerate(dils):
        n = tm // dil
        for half in range(GROUP_COLS // V7X_LANES):
            c0 = g * GROUP_COLS + half * V7X_LANES
            src = slice(c0, c0 + V7X_LANES)
            vsrc = slice(N_ATTN + c0, N_ATTN + c0 + V7X_LANES)
            pieces = ((q_refs[g], half * V7X_LANES, rope(yq[:, src])),
                      (kv_refs[g], half * V7X_LANES, rope(ykv[:, src])),
                      (kv_refs[g], GROUP_COLS + half * V7X_LANES, ykv[:, vsrc]))
            for out_ref, col, val in pieces:
                cols = slice(col, col + V7X_LANES)
                if dil == 1:
                    out_ref[0, :, cols] = val.astype(out_ref.dtype)
                else:
                    slab[slab_id] = val
                    for r in range(dil):
                        out_ref[r, :, cols] = slab[slab_id, pl.ds(r, n, stride=dil), :].astype(out_ref.dtype)
                    slab_id += 1


def kvq_proj(x, batch, seq_len, nkv, nq, wkv, wq, cos_t, sin_t, dils, tm):
    n, d = x.shape
    tps = seq_len // tm
    assert all(tm % dil == 0 and (tm // dil) % BF16_SUBLANES == 0 for dil in dils)
    n_slabs = max(1, 3 * (GROUP_COLS // V7X_LANES) * sum(dil > 1 for dil in dils))
    out_specs, out_shape = [], []
    for cols, dt in ((GROUP_COLS, BF16), (2 * GROUP_COLS, F32)):
        for dil in dils:
            out_specs.append(pl.BlockSpec((None, dil, tm // dil, cols), lambda b, i: (b, 0, i, 0)))
            out_shape.append(jax.ShapeDtypeStruct((batch, dil, seq_len // dil, cols), dt))
    return pl.pallas_call(
        functools.partial(_kvq_kernel, dils=dils),
        grid=(batch, tps),
        in_specs=[pl.BlockSpec((tm, d), lambda b, i: (b * tps + i, 0)), _const_spec((1, d)), _const_spec((1, d)),
                  _const_spec(wkv.shape), _const_spec(wq.shape),
                  pl.BlockSpec((tm, V7X_LANES), lambda b, i: (i, 0)),
                  pl.BlockSpec((tm, V7X_LANES), lambda b, i: (i, 0))],
        out_specs=out_specs,
        out_shape=out_shape,
        scratch_shapes=[pltpu.VMEM((n_slabs, tm, V7X_LANES), F32)],
        compiler_params=_params("parallel", "parallel"),
        name="kvq_proj",
    )(x, nkv, nq, wkv, wq, cos_t, sin_t)


def rope_tables(pos):
    half = ROT_DIM // 2
    inv_freq = ROPE_THETA ** (-jnp.arange(half, dtype=F32) / half)
    ang = pos.astype(F32)[:, None] * inv_freq[None, :]
    cos, sin = jnp.cos(ang), jnp.sin(ang)
    ones, zeros = jnp.ones_like(cos), jnp.zeros_like(sin)
    pad = (HEAD_DIM - ROT_DIM) // half
    cos_h = jnp.concatenate([cos, cos] + [ones] * pad, axis=1)
    sin_h = jnp.concatenate([-sin, sin] + [zeros] * pad, axis=1)
    reps = V7X_LANES // HEAD_DIM
    return jnp.tile(cos_h, (1, reps)), jnp.tile(sin_h, (1, reps))


def _head_masks(shape):
    lane = lax.broadcasted_iota(jnp.int32, shape, 1)
    return [(lane // HEAD_DIM) == h for h in range(HEADS_PER_GROUP)]


def _attn_prompt_kernel(q_ref, kvc_ref, kvp_ref, o_ref, l_ref, *, n_keys):
    i = pl.program_id(2)
    tq = n_keys
    n_sub = q_ref.shape[0] // tq
    kv = jnp.concatenate([kvp_ref[...], kvc_ref[...]], axis=0)
    k = kv[:, 0:GROUP_COLS].astype(BF16)
    v = kv[:, GROUP_COLS:2 * GROUP_COLS].astype(BF16)
    hm_q = _head_masks((tq, GROUP_COLS))
    hm_v = _head_masks((2 * tq, GROUP_COLS))
    qi = lax.broadcasted_iota(jnp.int32, (tq, 2 * tq), 0)
    kj = lax.broadcasted_iota(jnp.int32, (tq, 2 * tq), 1)
    rel = qi + tq - kj
    band = (rel >= 0) & (rel <= n_keys)
    bias = jnp.where(band, 0.0, -jnp.inf)
    bias_start = jnp.where(band & (kj >= tq), 0.0, -jnp.inf)
    bias_all = jnp.concatenate([bias] * HEADS_PER_GROUP, axis=0)
    bias_first = jnp.where(i == 0, jnp.concatenate([bias_start] * HEADS_PER_GROUP, axis=0), bias_all)
    for jb in range(n_sub):
        q = q_ref[jb * tq:(jb + 1) * tq, :]
        kb = k[jb * tq:(jb + 2) * tq]
        vb = v[jb * tq:(jb + 2) * tq]
        qs = jnp.concatenate([jnp.where(m, q, jnp.zeros_like(q)) for m in hm_q], axis=0)
        s = _dot_nt(qs, kb) * (HEAD_DIM ** -0.5) + (bias_first if jb == 0 else bias_all)
        m = jnp.max(s, axis=-1, keepdims=True)
        p = jnp.exp(s - m)
        l = jnp.sum(p, axis=-1, keepdims=True)
        pn = (p / l).astype(BF16)
        lse = m + jnp.log(l)
        o = jnp.zeros((tq, GROUP_COLS), F32)
        lmap = jnp.zeros((tq, GROUP_COLS), F32)
        for h in range(HEADS_PER_GROUP):
            rows = slice(h * tq, (h + 1) * tq)
            o = o + _dot(pn[rows], jnp.where(hm_v[h], vb, jnp.zeros_like(vb)))
            lmap = jnp.where(hm_q[h], lse[rows], lmap)
        o_ref[jb * tq:(jb + 1) * tq, :] = o
        l_ref[jb * tq:(jb + 1) * tq, :] = lmap


def attn_prompt(q, kv, group):
    window, dil = GROUPS[group]
    n_keys = window // dil
    batch, _, L, _ = q.shape
    tqb = _row_tile(L, ATTN_Q_ROWS)
    assert tqb % n_keys == 0
    per = tqb // n_keys
    cur = lambda b, r, i: (b, r, i, 0)
    prev = lambda b, r, i: (b, r, jnp.maximum(i * per - 1, 0), 0)
    return pl.pallas_call(
        functools.partial(_attn_prompt_kernel, n_keys=n_keys),
        grid=(batch, dil, L // tqb),
        in_specs=[pl.BlockSpec((None, None, tqb, GROUP_COLS), cur),
                  pl.BlockSpec((None, None, tqb, 2 * GROUP_COLS), cur),
                  pl.BlockSpec((None, None, n_keys, 2 * GROUP_COLS), prev)],
        out_specs=[pl.BlockSpec((None, None, tqb, GROUP_COLS), cur)] * 2,
        out_shape=[jax.ShapeDtypeStruct((batch, dil, L, GROUP_COLS), F32)] * 2,
        compiler_params=_params("parallel", "parallel", "parallel"),
        name=f"attn_prompt_g{group}",
    )(q, kv, kv)


def _attn_sample_kernel(*refs, n_new, groups):
    q_refs, kvn_refs, c_refs = refs[0:3], refs[3:6], refs[6:9]
    o_refs, l_refs, k_all, v_all = refs[9:12], refs[12:15], refs[15], refs[16]
    nq = q_refs[0].shape[0]
    for g in range(N_GROUPS):
        window, dil = groups[g]
        c_ref = c_refs[g]
        buf = c_ref.shape[0]
        n_all = buf + V7X_LANES
        k_all[0:buf, :] = c_ref[:, 0:GROUP_COLS]
        v_all[0:buf, :] = c_ref[:, GROUP_COLS:2 * GROUP_COLS]
        k_all[buf:n_all, :] = jnp.zeros((V7X_LANES, GROUP_COLS), F32)
        v_all[buf:n_all, :] = jnp.zeros((V7X_LANES, GROUP_COLS), F32)
        k_all[buf:buf + n_new, :] = kvn_refs[g][:, 0:GROUP_COLS]
        v_all[buf:buf + n_new, :] = kvn_refs[g][:, GROUP_COLS:2 * GROUP_COLS]
        k = k_all[0:n_all, :].astype(BF16)
        v = v_all[0:n_all, :].astype(BF16)
        q = q_refs[g][...]
        t = lax.broadcasted_iota(jnp.int32, (nq, n_all), 0)
        key = lax.broadcasted_iota(jnp.int32, (nq, n_all), 1)
        rel = buf + t - key
        ok = (rel >= 0) & ((rel & (dil - 1)) == 0) & (rel <= window) & (key < buf + n_new)
        hm_q = _head_masks(q.shape)
        hm_v = _head_masks(v.shape)
        qf = q.astype(F32)
        qs = jnp.concatenate([jnp.where(hm, qf, 0.0) for hm in hm_q], axis=0).astype(BF16)
        bias = jnp.where(ok, 0.0, -jnp.inf)
        s = _dot_nt(qs, k) * (HEAD_DIM ** -0.5) + jnp.concatenate([bias] * HEADS_PER_GROUP, axis=0)
        m = jnp.max(s, axis=-1, keepdims=True)
        p = jnp.exp(s - m)
        l = jnp.sum(p, axis=-1, keepdims=True)
        pn = p / l
        lse = m + jnp.log(l)
        o = jnp.zeros((nq, GROUP_COLS), F32)
        lmap = jnp.zeros((nq, GROUP_COLS), F32)
        for h in range(HEADS_PER_GROUP):
            rows = slice(h * nq, (h + 1) * nq)
            o = o + _dot(pn[rows].astype(BF16), jnp.where(hm_v[h], v, jnp.zeros_like(v)))
            lmap = jnp.where(hm_q[h], lse[rows], lmap)
        o_refs[g][...] = o
        l_refs[g][...] = lmap


def attn_sample(qs, kv_news, caches, n_new):
    nb, nq, _ = qs[0].shape
    keep = pl.next_power_of_2(n_new)
    groups, compact = [], []
    for c, (window, dil) in zip(caches, GROUPS):
        if dil > keep:
            c = c.reshape(nb, window // dil, dil, *c.shape[2:])[:, :, :keep]
            window, dil = window // dil * keep, keep
        groups.append((window, dil))
        compact.append(c.reshape(nb, window, 2 * GROUP_COLS))
    caches = compact
    max_buf = max(c.shape[1] for c in caches)
    per_b = lambda rows, cols: pl.BlockSpec((None, rows, cols), lambda b: (b, 0, 0))
    in_specs = ([per_b(nq, GROUP_COLS)] * N_GROUPS + [per_b(n_new, 2 * GROUP_COLS)] * N_GROUPS
                + [per_b(c.shape[1], 2 * GROUP_COLS) for c in caches])
    outs = pl.pallas_call(
        functools.partial(_attn_sample_kernel, n_new=n_new, groups=tuple(groups)),
        grid=(nb,),
        in_specs=in_specs,
        out_specs=[per_b(nq, GROUP_COLS)] * (2 * N_GROUPS),
        out_shape=[jax.ShapeDtypeStruct((nb, nq, GROUP_COLS), F32)] * (2 * N_GROUPS),
        scratch_shapes=[pltpu.VMEM((max_buf + V7X_LANES, GROUP_COLS), F32)] * 2,
        compiler_params=_params("parallel"),
        name="attn_sample",
    )(*qs, *kv_news, *caches)
    return outs[0:N_GROUPS], outs[N_GROUPS:2 * N_GROUPS]


def _combine_wo_kernel(*refs, dils, n_experts):
    o_refs, l_refs = refs[0:N_GROUPS], refs[N_GROUPS:2 * N_GROUPS]
    x_ref, w_ref, nw_ref, wr_ref, out_ref, xn_ref, meta_ref, metat_ref, cnt_ref, slab = refs[2 * N_GROUPS:]
    tm = x_ref.shape[0]
    slab_id = 0

    def in_position_order(ref, dil):
        nonlocal slab_id
        halves = []
        for half in range(GROUP_COLS // V7X_LANES):
            cols = slice(half * V7X_LANES, (half + 1) * V7X_LANES)
            if dil == 1:
                halves.append(ref[0, :, cols])
            else:
                for r in range(dil):
                    slab[slab_id, pl.ds(r, tm // dil, stride=dil), :] = ref[r, :, cols]
                halves.append(slab[slab_id])
                slab_id += 1
        return jnp.concatenate(halves, axis=1)

    ls = [in_position_order(l_refs[g], dils[g]) for g in range(N_GROUPS)]
    mx = jnp.maximum(jnp.maximum(ls[0], ls[1]), ls[2])
    es = [jnp.exp(l - mx) for l in ls]
    tot = es[0] + es[1] + es[2]
    acc = x_ref[...]
    for g in range(N_GROUPS):
        og = (in_position_order(o_refs[g], dils[g]) * (es[g] / tot)).astype(BF16)
        acc = acc + _dot(og, w_ref[g * GROUP_COLS:(g + 1) * GROUP_COLS, :])
    out_ref[...] = acc
    xn_ref[...], meta_ref[...], metat_ref[...], cnt_ref[...] = _route(acc, nw_ref[...], wr_ref[...], n_experts)


def combine_wo(os_, ls, x, batch, seq_len, w_o, nw, w_router, dils, tm):
    n, d = x.shape
    tps = seq_len // tm
    n_experts = w_router.shape[1]
    wr = jnp.pad(w_router, ((0, 0), (0, V7X_LANES - n_experts)))
    gspecs = [pl.BlockSpec((None, dil, tm // dil, GROUP_COLS), lambda b, i: (b, 0, i, 0)) for dil in dils]
    n_slabs = max(1, 2 * (GROUP_COLS // V7X_LANES) * sum(dil > 1 for dil in dils))
    rows = lambda cols: pl.BlockSpec((tm, cols), lambda b, i: (b * tps + i, 0))
    per_tile = lambda r, c: pl.BlockSpec((None, r, c), lambda b, i: (b * tps + i, 0, 0))
    return pl.pallas_call(
        functools.partial(_combine_wo_kernel, dils=dils, n_experts=n_experts),
        grid=(batch, tps),
        in_specs=gspecs + gspecs + [rows(d), _const_spec(w_o.shape), _const_spec((1, d)), _const_spec(wr.shape)],
        out_specs=[rows(d), rows(d), rows(META_COLS), per_tile(META_COLS, tm), per_tile(1, V7X_LANES)],
        out_shape=[jax.ShapeDtypeStruct((n, d), F32), jax.ShapeDtypeStruct((n, d), BF16),
                   jax.ShapeDtypeStruct((n, META_COLS), F32),
                   jax.ShapeDtypeStruct((n // tm, META_COLS, tm), F32),
                   jax.ShapeDtypeStruct((n // tm, 1, V7X_LANES), jnp.int32)],
        scratch_shapes=[pltpu.VMEM((n_slabs, tm, V7X_LANES), F32)],
        compiler_params=_params("parallel", "parallel"),
        name="combine_wo",
    )(*os_, *ls, x, w_o, nw, wr)


def _route(x, nw, wr, n_experts):
    xn = _rms(x, nw)
    xh = xn.astype(BF16)
    xl = (xn - xh.astype(F32)).astype(BF16)
    wh = wr.astype(BF16)
    wl = (wr - wh.astype(F32)).astype(BF16)
    logits = _dot(xh, wh) + (_dot(xl, wh) + _dot(xh, wl))
    lane = lax.broadcasted_iota(jnp.int32, logits.shape, 1)
    logits = jnp.where(lane < n_experts, logits, -jnp.inf)
    v1 = jnp.max(logits, axis=-1, keepdims=True)
    i1 = jnp.min(jnp.where(logits == v1, lane, V7X_LANES), axis=-1, keepdims=True)
    rest = jnp.where(lane == i1, -jnp.inf, logits)
    v2 = jnp.max(rest, axis=-1, keepdims=True)
    i2 = jnp.min(jnp.where(rest == v2, lane, V7X_LANES), axis=-1, keepdims=True)
    e2 = jnp.exp(v2 - v1)
    den = 1.0 + e2
    g1, g2 = 1.0 / den, e2 / den
    onehot = ((lane == i1) | (lane == i2)).astype(BF16)
    tm = onehot.shape[0]
    tri = (lax.broadcasted_iota(jnp.int32, (tm, tm), 0) >= lax.broadcasted_iota(jnp.int32, (tm, tm), 1)).astype(BF16)
    csum = _dot(tri, onehot)
    r1 = jnp.sum(jnp.where(lane == i1, csum, 0.0), axis=-1, keepdims=True) - 1.0
    r2 = jnp.sum(jnp.where(lane == i2, csum, 0.0), axis=-1, keepdims=True) - 1.0
    cols = (i1.astype(F32), i2.astype(F32), r1, r2, g1, g2)
    meta = jnp.zeros(logits.shape, F32)
    for ci, col in enumerate(cols):
        meta = jnp.where(lane == ci, col, meta)
    return xn.astype(BF16), meta[:, :META_COLS], meta.T[:META_COLS, :], csum[tm - 1:tm, :].astype(jnp.int32)


def _segment_bits(tm):
    bits, b = [], SEG_ALIGN
    while b <= tm:
        bits.append(b)
        b *= 2
    return bits[::-1]


def _compact_rows(tm, n_experts):
    worst = TOP_K * tm + n_experts * (SEG_ALIGN - 1)
    return _round_up(worst, BF16_SUBLANES)


def _segment_copies(seg_sm, base_sm, tile, n_experts, tm, hbm_ref, vmem_ref, sems, to_hbm):
    out = []
    local = 0
    for e in range(n_experts):
        n = seg_sm[tile * n_experts + e]
        base = base_sm[tile * n_experts + e]
        for b, bit in enumerate(_segment_bits(tm)):
            done = n & ~(2 * bit - 1)
            src = vmem_ref.at[pl.ds(pl.multiple_of(local + done, SEG_ALIGN), bit)]
            dst = hbm_ref.at[pl.ds(pl.multiple_of(base + done, SEG_ALIGN), bit)]
            if not to_hbm:
                src, dst = dst, src
            out.append(((n & bit) != 0, pltpu.make_async_copy(src, dst, sems.at[e, b])))
        local = local + n
    return out, local


def _token_dest(e_k, r_k, seg_sm, tile, n_experts):
    dest = r_k
    local = 0
    for e in range(n_experts):
        dest = dest + jnp.where(e_k == e, local, 0)
        local = local + seg_sm[tile * n_experts + e]
    return dest


def _start_all(copies):
    for cond, cp in copies:
        pl.when(cond)(cp.start)


def _wait_all(copies):
    for cond, cp in copies:
        pl.when(cond)(cp.wait)


def _zero_copies(zn_sm, zbase_sm, n_spans, zero_rows, xs_hbm, zero_ref, zero_sems):
    out = []
    for k in range(n_spans):
        n, base = zn_sm[k], zbase_sm[k]
        for b, bit in enumerate(_segment_bits(zero_rows)):
            done = n & ~(2 * bit - 1)
            dst = xs_hbm.at[pl.ds(pl.multiple_of(base + done, SEG_ALIGN), bit)]
            out.append(((n & bit) != 0, pltpu.make_async_copy(zero_ref.at[pl.ds(0, bit)], dst, zero_sems.at[k, b])))
    return out


def _dispatch_kernel(seg_sm, base_sm, zn_sm, zbase_sm, tail_sm, xn_ref, metat_ref, *refs,
                     n_experts, n_tiles, tile0, n_spans, tmg, first_call):
    xs_hbm, comp_ref, zero_ref, sems, zero_sems, tail_sems = refs[-6:]
    i = pl.program_id(0)
    tile, slot = tile0 + i, i % 2
    tm, d = xn_ref.shape
    rows = comp_ref.shape[1]
    zero_rows = zero_ref.shape[0]

    def copies(t, s):
        return _segment_copies(seg_sm, base_sm, t, n_experts, tm, xs_hbm, comp_ref.at[s], sems.at[s], to_hbm=True)[0]

    if first_call:
        zeros = _zero_copies(zn_sm, zbase_sm, n_spans, zero_rows, xs_hbm, zero_ref, zero_sems)

        @pl.when(i == 0)
        def _():
            zero_ref[...] = jnp.zeros_like(zero_ref)
            _start_all(zeros)

    @pl.when(i < n_tiles)
    def _():
        mt = metat_ref[...]
        as_int = lambda v: v.astype(jnp.int32)
        dest1 = _token_dest(as_int(mt[0:1, :]), as_int(mt[2:3, :]), seg_sm, tile, n_experts)
        dest2 = _token_dest(as_int(mt[1:2, :]), as_int(mt[3:4, :]), seg_sm, tile, n_experts)
        row_id = lax.broadcasted_iota(jnp.int32, (rows, tm), 0)
        p1, p2 = row_id == dest1, row_id == dest2
        comp_ref[slot, :, 0:d] = _dot((p1 | p2).astype(BF16), xn_ref[...])
        gate = jnp.sum(jnp.where(p1, mt[4:5, :], 0.0) + jnp.where(p2, mt[5:6, :], 0.0), axis=-1, keepdims=True)
        comp_ref[slot, :, d:d + V7X_LANES] = jnp.broadcast_to(gate, (rows, V7X_LANES))
        _start_all(copies(tile, slot))

    @pl.when((i > 0) & (i <= n_tiles))
    def _():
        _wait_all(copies(tile - 1, 1 - slot))

    if first_call:
        @pl.when(i == n_tiles)
        def _():
            _wait_all(zeros)

        @pl.when((i > n_tiles) & (i - n_tiles - 1 < tail_sm[1]))
        def _():
            start = tail_sm[0] + (i - n_tiles - 1) * tmg
            parts = [pltpu.make_async_copy(
                zero_ref, xs_hbm.at[pl.ds(pl.multiple_of(start + p * zero_rows, SEG_ALIGN), zero_rows)], tail_sems.at[p])
                for p in range(tmg // zero_rows)]
            for cp in parts:
                cp.start()
            for cp in parts:
                cp.wait()


def dispatch(xn, metat, tables, n_experts, total_rows, tail_tiles, tm, tmg, tile0, xs_prev):
    n, d = xn.shape
    n_tiles = n // tm
    n_spans = tables[2].shape[0]
    rows = _compact_rows(tm, n_experts)
    zero_rows = tmg // 2
    first_call = xs_prev is None
    last_tile = lambda i: jnp.minimum(i, n_tiles - 1)
    in_specs = [pl.BlockSpec((tm, d), lambda i, *_: (last_tile(i), 0)),
                pl.BlockSpec((None, META_COLS, tm), lambda i, *_: (last_tile(i), 0, 0))]
    args = [*tables, xn, metat]
    if not first_call:
        in_specs.append(pl.BlockSpec(memory_space=pl.ANY))
        args.append(xs_prev)
    grid_spec = pltpu.PrefetchScalarGridSpec(
        num_scalar_prefetch=len(tables),
        grid=(n_tiles + 1 + (tail_tiles if first_call else 0),),
        in_specs=in_specs,
        out_specs=pl.BlockSpec(memory_space=pl.ANY),
        scratch_shapes=[pltpu.VMEM((2, rows, d + V7X_LANES), F32),
                        pltpu.VMEM((zero_rows, d + V7X_LANES), F32),
                        pltpu.SemaphoreType.DMA((2, n_experts, len(_segment_bits(tm)))),
                        pltpu.SemaphoreType.DMA((n_spans, len(_segment_bits(zero_rows)))),
                        pltpu.SemaphoreType.DMA((tmg // zero_rows,))],
    )
    return pl.pallas_call(
        functools.partial(_dispatch_kernel, n_experts=n_experts, n_tiles=n_tiles, tile0=tile0, n_spans=n_spans,
                          tmg=tmg, first_call=first_call),
        grid_spec=grid_spec,
        out_shape=jax.ShapeDtypeStruct((total_rows, d + V7X_LANES), F32),
        input_output_aliases={} if first_call else {len(args) - 1: 0},
        compiler_params=_params("arbitrary"),
        name="moe_dispatch",
    )(*args)


def _expert_ffn_kernel(te_sm, nu_sm, xs_ref, wg_ref, wu_ref, wd_ref, ys_ref, xb_ref, acc_ref):
    del te_sm
    j, c = pl.program_id(0), pl.program_id(1)
    last_c = pl.num_programs(1) - 1
    d = xb_ref.shape[1]
    used = j < nu_sm[0]

    @pl.when(used & (c == 0))
    def _():
        xb_ref[...] = xs_ref[:, 0:d].astype(BF16)
        acc_ref[...] = jnp.zeros_like(acc_ref)

    @pl.when(used)
    def _():
        xb = xb_ref[...]
        h = _silu(_dot(xb, wg_ref[...])) * _dot(xb, wu_ref[...])
        acc_ref[...] += _dot(h.astype(BF16), wd_ref[...])

    @pl.when(used & (c == last_c))
    def _():
        ys_ref[...] = xs_ref[:, d:d + 1] * acc_ref[...]

    @pl.when(jnp.logical_not(used) & (c == last_c))
    def _():
        ys_ref[...] = jnp.zeros_like(ys_ref)


def expert_ffn(xs, tile_expert, n_used, wg, wu, wd, tmg):
    total_rows, dx = xs.shape
    n_experts, d, de = wg.shape
    tf = next(c for c in (MOE_FF_CHUNK, MOE_FF_CHUNK // 2, _ff_chunk(de)) if de % c == 0)
    last_used = lambda j, nu: jnp.minimum(j, nu[0] - 1)
    grid_spec = pltpu.PrefetchScalarGridSpec(
        num_scalar_prefetch=2,
        grid=(total_rows // tmg, de // tf),
        in_specs=[pl.BlockSpec((tmg, dx), lambda j, c, te, nu: (last_used(j, nu), 0)),
                  pl.BlockSpec((None, d, tf), lambda j, c, te, nu: (te[j], 0, c)),
                  pl.BlockSpec((None, d, tf), lambda j, c, te, nu: (te[j], 0, c)),
                  pl.BlockSpec((None, tf, d), lambda j, c, te, nu: (te[j], c, 0))],
        out_specs=pl.BlockSpec((tmg, d), lambda j, c, te, nu: (j, 0)),
        scratch_shapes=[pltpu.VMEM((tmg, d), BF16), pltpu.VMEM((tmg, d), F32)],
    )
    return pl.pallas_call(
        _expert_ffn_kernel,
        grid_spec=grid_spec,
        out_shape=jax.ShapeDtypeStruct((total_rows, d), F32),
        compiler_params=_params("arbitrary", "arbitrary"),
        name="moe_expert_ffn",
    )(tile_expert, n_used, xs, wg, wu, wd)


def _combine_kernel(seg_sm, base_sm, x_ref, meta_ref, nf_ref, ys_hbm, y_ref, comp_ref, sems, *, n_experts, tile0):
    i, n_steps = pl.program_id(0), pl.num_programs(0)
    tile, slot = tile0 + i, i % 2
    tm, d = x_ref.shape
    rows = comp_ref.shape[1]

    def copies(t, s):
        return _segment_copies(seg_sm, base_sm, t, n_experts, tm, ys_hbm, comp_ref.at[s], sems.at[s], to_hbm=False)

    @pl.when(i == 0)
    def _():
        _start_all(copies(tile, slot)[0])

    @pl.when(i + 1 < n_steps)
    def _():
        _start_all(copies(tile + 1, 1 - slot)[0])

    meta = meta_ref[...]
    as_int = lambda v: v.astype(jnp.int32)
    dest1 = _token_dest(as_int(meta[:, 0:1]), as_int(meta[:, 2:3]), seg_sm, tile, n_experts)
    dest2 = _token_dest(as_int(meta[:, 1:2]), as_int(meta[:, 3:4]), seg_sm, tile, n_experts)
    col_id = lax.broadcasted_iota(jnp.int32, (tm, rows), 1)
    pick = ((col_id == dest1) | (col_id == dest2)).astype(BF16)
    mine, n_rows = copies(tile, slot)
    _wait_all(mine)
    live = lax.broadcasted_iota(jnp.int32, (rows, 1), 0) < n_rows
    ys = jnp.where(live, comp_ref[slot], 0.0)
    hi = ys.astype(BF16)
    lo = (ys - hi.astype(F32)).astype(BF16)
    y_ref[...] = _rms(x_ref[...] + (_dot(pick, hi) + _dot(pick, lo)), nf_ref[...])


def combine_final(x, meta, nf, ys, seg, base, n_experts, tm, tile0):
    n, d = x.shape
    rows = _compact_rows(tm, n_experts)
    grid_spec = pltpu.PrefetchScalarGridSpec(
        num_scalar_prefetch=2,
        grid=(n // tm,),
        in_specs=[pl.BlockSpec((tm, d), lambda i, *_: (i, 0)),
                  pl.BlockSpec((tm, META_COLS), lambda i, *_: (i, 0)),
                  pl.BlockSpec((1, d), lambda i, *_: (0, 0)),
                  pl.BlockSpec(memory_space=pl.ANY)],
        out_specs=pl.BlockSpec((tm, d), lambda i, *_: (i, 0)),
        scratch_shapes=[pltpu.VMEM((2, rows, d), F32),
                        pltpu.SemaphoreType.DMA((2, n_experts, len(_segment_bits(tm))))],
    )
    return pl.pallas_call(
        functools.partial(_combine_kernel, n_experts=n_experts, tile0=tile0),
        grid_spec=grid_spec,
        out_shape=jax.ShapeDtypeStruct((n, d), F32),
        compiler_params=_params("arbitrary"),
        name="moe_combine",
    )(seg, base, x, meta, nf, ys)


def moe_final(xs, routed, tms, nf, wg, wu, wd, tmg):
    n_experts = wg.shape[0]
    n_tiles = [x.shape[0] // tm for x, tm in zip(xs, tms)]
    n_tokens = sum(x.shape[0] for x in xs)
    cnt = jnp.concatenate([r[3][:, 0, :n_experts] for r in routed], axis=0)
    seg = _round_up(cnt, SEG_ALIGN)
    within = jnp.cumsum(seg, axis=0) - seg
    exp_live = jnp.sum(seg, axis=0)
    exp_rows = _round_up(exp_live, tmg)
    exp_end = jnp.cumsum(exp_rows)
    base = (exp_end - exp_rows)[None, :] + within
    total_rows = _round_up(TOP_K * n_tokens + sum(n_tiles) * n_experts * (SEG_ALIGN - 1)
                           + n_experts * (tmg - SEG_ALIGN), tmg)
    n_used = (exp_end[-1] // tmg).reshape(1).astype(jnp.int32)
    tile_start = jnp.arange(total_rows // tmg, dtype=jnp.int32) * tmg
    tile_expert = jnp.minimum(jnp.sum(tile_start[:, None] >= exp_end[None, :], axis=1), n_experts - 1).astype(jnp.int32)
    i32 = lambda a: a.reshape(-1).astype(jnp.int32)
    zero_n = jnp.concatenate([exp_rows - exp_live, seg[n_tiles[0]:].reshape(-1)])
    zero_base = jnp.concatenate([exp_end - exp_rows + exp_live, base[n_tiles[0]:].reshape(-1)])
    tail = jnp.stack([exp_end[-1], (total_rows - exp_end[-1]) // tmg])
    tail_tiles = (total_rows - TOP_K * n_tokens) // tmg
    tables = (i32(seg), i32(base), i32(zero_n), i32(zero_base), i32(tail))
    rows_buf, tile0 = None, 0
    for (xn, _, metat, _), x, tm, nt in zip(routed, xs, tms, n_tiles):
        rows_buf = dispatch(xn, metat, tables, n_experts, total_rows, tail_tiles, tm, tmg, tile0, rows_buf)
        tile0 += nt
    ys = expert_ffn(rows_buf, tile_expert, n_used, wg, wu, wd, tmg)
    outs, tile0 = [], 0
    for (_, meta, _, _), x, tm, nt in zip(routed, xs, tms, n_tiles):
        outs.append(combine_final(x, meta, nf, ys, tables[0], tables[1], n_experts, tm, tile0))
        tile0 += nt
    return outs


def kernel(x_prompt, x_sample, state_conv, cache_kv_g0, cache_kv_g1, cache_kv_g2, norm_mix, norm_ffn, norm_kv, norm_final, w_pw1, b_pw1, w_dw, b_dw, ln_conv_g, ln_conv_b, w_pw2, b_pw2, w_q, w_kv, w_o, w_gate_dense, w_up_dense, w_down_dense, w_router, w_gate_exp, w_up_exp, w_down_exp):
    batch, seq_len, d = x_prompt.shape
    dec_batch, dec_seq, _ = x_sample.shape
    caches = (cache_kv_g0, cache_kv_g1, cache_kv_g2)
    assert norm_mix.shape[0] == 2 and state_conv.shape[0] == 1 and w_q.shape[0] == 1
    assert all(c.shape[1] == w for c, (w, _) in zip(caches, GROUPS))
    assert dec_seq <= SAMPLE_Q_ROWS

    row = lambda v: v.reshape(1, -1).astype(F32)
    bf = lambda w: w.astype(BF16)
    n_p = batch * seq_len
    n_s = dec_batch * dec_seq
    tm_p = _row_tile(seq_len, 512)
    tm_s = n_s
    dils = tuple(dil for _, dil in GROUPS)
    no_dil = (1,) * N_GROUPS

    wdw = jnp.pad(w_dw[0], ((0, TAP_ROWS - CONV_WIDTH), (0, 0)))
    conv_w = (wdw, row(b_dw[0]), row(ln_conv_g[0]), row(ln_conv_b[0]), bf(w_pw2[0]), row(b_pw2[0]))
    w_pw1_b, b_pw1_r = bf(w_pw1[0]), row(b_pw1[0])
    ffn_w = (bf(w_gate_dense[0]), bf(w_up_dense[0]), bf(w_down_dense[0]))
    kvq_w = (row(norm_kv), row(norm_mix[1]), bf(w_kv), bf(w_q[0]))
    w_o_b = bf(w_o[0])
    moe_w = (bf(w_gate_exp[0]), bf(w_up_exp[0]), bf(w_down_exp[0]))

    cos_p, sin_p = rope_tables(jnp.arange(seq_len))
    cos_s, sin_s = rope_tables(PAST_LEN + jnp.arange(n_s) // dec_batch)

    hp = x_prompt.reshape(n_p, d)
    hs = x_sample.transpose(1, 0, 2).reshape(n_s, d)
    g_p = pw1_glu(hp, row(norm_mix[0]), w_pw1_b, b_pw1_r, tm_p)
    g_s = pw1_glu(hs, row(norm_mix[0]), w_pw1_b, b_pw1_r, tm_s)
    wdw_rep = jnp.broadcast_to(w_dw[0][:, None, :], (CONV_WIDTH, V7X_SUBLANES, d))
    hp = conv_prompt(g_p, hp, seq_len, wdw_rep, *conv_w[1:], tm_p)
    full_s = jnp.concatenate([state_conv[0].transpose(1, 0, 2), g_s.reshape(dec_seq, dec_batch, d)], axis=0)
    hs = conv_sample(full_s, hs, *conv_w)
    conv_prompt_out = g_p.reshape(batch, seq_len, d)[:, seq_len - (CONV_WIDTH - 1):][None]
    conv_sample_out = full_s[dec_seq:].transpose(1, 0, 2)[None]

    hp = ffn(hp, row(norm_ffn[0]), *ffn_w, tm_p)
    hs = ffn(hs, row(norm_ffn[0]), *ffn_w, tm_s)

    kvq_p = kvq_proj(hp, batch, seq_len, *kvq_w, cos_p, sin_p, dils, tm_p)
    q_p, kv_p = kvq_p[0:N_GROUPS], kvq_p[N_GROUPS:]
    kvq_s = kvq_proj(hs, 1, n_s, *kvq_w, cos_s, sin_s, no_dil, tm_s)
    to_bm = lambda a: a.reshape(dec_seq, dec_batch, -1).transpose(1, 0, 2)
    q_s = [jnp.pad(to_bm(a), ((0, 0), (0, SAMPLE_Q_ROWS - dec_seq), (0, 0))) for a in kvq_s[0:N_GROUPS]]
    kv_s = [to_bm(a) for a in kvq_s[N_GROUPS:]]

    att_p = [attn_prompt(q_p[g], kv_p[g], g) for g in range(N_GROUPS)]
    route_w = (w_o_b, row(norm_ffn[1]), w_router[0])
    hp, *routed_p = combine_wo([a[0] for a in att_p], [a[1] for a in att_p], hp, batch, seq_len, *route_w, dils, tm_p)

    hs = to_bm(hs).reshape(n_s, d)
    o_s, l_s = attn_sample(q_s, kv_s, caches, dec_seq)
    as_rows = lambda a: a[:, :dec_seq].reshape(1, 1, n_s, GROUP_COLS)
    hs, *routed_s = combine_wo([as_rows(a) for a in o_s], [as_rows(a) for a in l_s], hs, 1, n_s, *route_w, no_dil, tm_s)

    y_p, y_s = moe_final([hp, hs], [routed_p, routed_s], [tm_p, tm_s], row(norm_final), *moe_w, MOE_ROW_TILE)

    kv_out = []
    for g, (window, dil) in enumerate(GROUPS):
        keep = min(window, seq_len)
        tail = kv_p[g][:, :, (seq_len - keep) // dil:, :]
        kv_out.append(tail.transpose(0, 2, 1, 3).reshape(batch, keep, 2, HEADS_PER_GROUP, HEAD_DIM))
        kv_out.append(kv_s[g].reshape(dec_batch, dec_seq, 2, HEADS_PER_GROUP, HEAD_DIM))
    return (y_p.reshape(batch, seq_len, d), y_s.reshape(dec_batch, dec_seq, d),
            conv_prompt_out, conv_sample_out, *kv_out)
```

```python
import functools

import jax
import jax.numpy as jnp
from jax import lax
from jax.experimental import pallas as pl
from jax.experimental.pallas import tpu as pltpu

F32 = jnp.float32
BF16 = jnp.bfloat16

EPS = 1e-5
CONV_WIDTH = 31
HEAD_DIM = 64
ROT_DIM = HEAD_DIM // 4
HEADS_PER_GROUP = 4
GROUP_COLS = HEADS_PER_GROUP * HEAD_DIM
GROUPS = ((128, 1), (512, 4), (2048, 16))
N_GROUPS = len(GROUPS)
N_ATTN = N_GROUPS * GROUP_COLS
ROPE_THETA = 500000.0
TOP_K = 2
PAST_LEN = 16384

V7X_LANES = 128
V7X_SUBLANES = 8
V7X_VMEM_BYTES = 64 * 2**20
VMEM_LIMIT_BYTES = V7X_VMEM_BYTES - 8 * 2**20
BF16_SUBLANES = 2 * V7X_SUBLANES

HALO_ROWS = 32
TAP_ROWS = 32
CONV_ROW_CHUNK = 32
ATTN_Q_ROWS = 512
SAMPLE_Q_ROWS = 8
META_COLS = 8
SEG_ALIGN = V7X_SUBLANES
MOE_ROW_TILE = 512
MOE_FF_CHUNK = 1792


def _params(*sem):
    return pltpu.CompilerParams(dimension_semantics=sem, vmem_limit_bytes=VMEM_LIMIT_BYTES)


def _rms_unit(x):
    return x * lax.rsqrt(jnp.mean(x * x, axis=-1, keepdims=True) + EPS)


def _rms(x, w):
    return _rms_unit(x) * w


def _silu(x):
    return x * jax.nn.sigmoid(x)


def _dot(a, b):
    return jnp.dot(a, b, preferred_element_type=F32)


def _dot_nt(a, b):
    return lax.dot_general(a, b, (((1,), (1,)), ((), ())), preferred_element_type=F32)


def _row_tile(n_rows, want):
    t = min(n_rows, want)
    assert n_rows % t == 0 and (t % V7X_SUBLANES == 0 or t == n_rows)
    return t


def _round_up(v, m):
    return (v + m - 1) // m * m


def _const_spec(shape):
    return pl.BlockSpec(shape, lambda *_: (0,) * len(shape), pipeline_mode=pl.Buffered(1))


def _pw1_glu_kernel(x_ref, nw_ref, w_ref, b_ref, g_ref):
    d = x_ref.shape[1]
    xn = _rms(x_ref[...], nw_ref[...]).astype(BF16)
    u = _dot(xn, w_ref[...]) + b_ref[...]
    g_ref[...] = u[:, :d] * jax.nn.sigmoid(u[:, d:])


def pw1_glu(x, nw, w, b, tm):
    n, d = x.shape
    return pl.pallas_call(
        _pw1_glu_kernel,
        grid=(n // tm,),
        in_specs=[pl.BlockSpec((tm, d), lambda i: (i, 0)),
                  _const_spec((1, d)), _const_spec((d, 2 * d)), _const_spec((1, 2 * d))],
        out_specs=pl.BlockSpec((tm, d), lambda i: (i, 0)),
        out_shape=jax.ShapeDtypeStruct((n, d), F32),
        compiler_params=_params("parallel"),
        name="pw1_glu",
    )(x, nw, w, b)


def _conv_tail(c, x, lng, lnb, w2, b2):
    mu = jnp.mean(c, axis=-1, keepdims=True)
    cc = c - mu
    y = cc * lax.rsqrt(jnp.mean(cc * cc, axis=-1, keepdims=True) + EPS)
    y = _silu(y * lng + lnb)
    return x + _dot(y.astype(BF16), w2) + b2


def _conv_prompt_kernel(g_ref, halo_ref, x_ref, wdw_ref, bdw_ref, lng_ref, lnb_ref, w2_ref, b2_ref,
                        o_ref, win_ref, c_ref, *, tiles_per_seq):
    tm, d = g_ref.shape
    first = (pl.program_id(0) % tiles_per_seq) == 0
    win_ref[0:HALO_ROWS, :] = jnp.where(first, 0.0, halo_ref[...])
    win_ref[HALO_ROWS:HALO_ROWS + tm, :] = g_ref[...]
    lead = HALO_ROWS - (CONV_WIDTH - 1)
    span = HALO_ROWS + CONV_ROW_CHUNK

    def chunk(ci, carry):
        r0 = pl.multiple_of(ci * CONV_ROW_CHUNK, CONV_ROW_CHUNK)
        w = win_ref[pl.ds(r0, span), :]
        acc = None
        for phase in range(V7X_SUBLANES):
            ws = w if phase == 0 else pltpu.roll(w, span - phase, axis=0)
            for off in range(phase, lead + CONV_WIDTH, V7X_SUBLANES):
                j = off - lead
                if j >= 0:
                    base = off - phase
                    rows = ws[base:base + CONV_ROW_CHUNK, :].reshape(CONV_ROW_CHUNK // V7X_SUBLANES, V7X_SUBLANES, d)
                    term = wdw_ref[j] * rows
                    acc = term if acc is None else acc + term
        c_ref[pl.ds(r0, CONV_ROW_CHUNK), :] = acc.reshape(CONV_ROW_CHUNK, d) + bdw_ref[...]
        return carry

    lax.fori_loop(0, tm // CONV_ROW_CHUNK, chunk, 0)
    o_ref[...] = _conv_tail(c_ref[...], x_ref[...], lng_ref[...], lnb_ref[...], w2_ref[...], b2_ref[...])


def conv_prompt(g, x, seq_len, wdw, bdw, lng, lnb, w2, b2, tm):
    n, d = g.shape
    tiles_per_seq = seq_len // tm
    halo_per_tile = tm // HALO_ROWS
    return pl.pallas_call(
        functools.partial(_conv_prompt_kernel, tiles_per_seq=tiles_per_seq),
        grid=(n // tm,),
        in_specs=[pl.BlockSpec((tm, d), lambda i: (i, 0)),
                  pl.BlockSpec((HALO_ROWS, d), lambda i: (jnp.maximum(i * halo_per_tile - 1, 0), 0)),
                  pl.BlockSpec((tm, d), lambda i: (i, 0)),
                  _const_spec(wdw.shape), _const_spec((1, d)), _const_spec((1, d)), _const_spec((1, d)),
                  _const_spec((d, d)), _const_spec((1, d))],
        out_specs=pl.BlockSpec((tm, d), lambda i: (i, 0)),
        out_shape=jax.ShapeDtypeStruct((n, d), F32),
        scratch_shapes=[pltpu.VMEM((HALO_ROWS + tm, d), F32), pltpu.VMEM((tm, d), F32)],
        compiler_params=_params("parallel"),
        name="conv_prompt",
    )(g, g, x, wdw, bdw, lng, lnb, w2, b2)


def _conv_sample_kernel(full_ref, x_ref, wdw_ref, bdw_ref, lng_ref, lnb_ref, w2_ref, b2_ref, o_ref, *, n_t):
    nb, d = full_ref.shape[1], full_ref.shape[2]
    for t in range(n_t):
        acc = jnp.broadcast_to(bdw_ref[...], (nb, d))
        for j in range(CONV_WIDTH):
            acc = acc + wdw_ref[j:j + 1, :] * full_ref[t + j]
        rows = slice(t * nb, (t + 1) * nb)
        o_ref[rows, :] = _conv_tail(acc, x_ref[rows, :], lng_ref[...], lnb_ref[...], w2_ref[...], b2_ref[...])


def conv_sample(full_tm, x, wdw, bdw, lng, lnb, w2, b2):
    rows_full, nb, d = full_tm.shape
    n_t = rows_full - (CONV_WIDTH - 1)
    n = n_t * nb
    return pl.pallas_call(
        functools.partial(_conv_sample_kernel, n_t=n_t),
        grid=(1,),
        in_specs=[_const_spec((rows_full, nb, d)), _const_spec((n, d)),
                  _const_spec((TAP_ROWS, d)), _const_spec((1, d)), _const_spec((1, d)), _const_spec((1, d)),
                  _const_spec((d, d)), _const_spec((1, d))],
        out_specs=pl.BlockSpec((n, d), lambda i: (0, 0)),
        out_shape=jax.ShapeDtypeStruct((n, d), F32),
        compiler_params=_params("arbitrary"),
        name="conv_sample",
    )(full_tm, x, wdw, bdw, lng, lnb, w2, b2)


def _ffn_kernel(x_ref, nw_ref, wg_ref, wu_ref, wd_ref, o_ref, h_ref, *, ff_chunk):
    ff = wg_ref.shape[1]
    x = x_ref[...]
    xn = _rms(x, nw_ref[...]).astype(BF16)
    for c in range(ff // ff_chunk):
        cols = slice(c * ff_chunk, (c + 1) * ff_chunk)
        h = _silu(_dot(xn, wg_ref[:, cols])) * _dot(xn, wu_ref[:, cols])
        h_ref[:, cols] = h.astype(BF16)
    o_ref[...] = x + _dot(h_ref[...], wd_ref[...])


def _ff_chunk(ff):
    for c in (512, 256, 128):
        if ff % c == 0:
            return c
    return ff


def ffn(x, nw, wg, wu, wd, tm):
    n, d = x.shape
    ff = wg.shape[1]
    return pl.pallas_call(
        functools.partial(_ffn_kernel, ff_chunk=_ff_chunk(ff)),
        grid=(n // tm,),
        in_specs=[pl.BlockSpec((tm, d), lambda i: (i, 0)), _const_spec((1, d)),
                  _const_spec((d, ff)), _const_spec((d, ff)), _const_spec((ff, d))],
        out_specs=pl.BlockSpec((tm, d), lambda i: (i, 0)),
        out_shape=jax.ShapeDtypeStruct((n, d), F32),
        scratch_shapes=[pltpu.VMEM((tm, ff), BF16)],
        compiler_params=_params("parallel"),
        name="ffn_dense",
    )(x, nw, wg, wu, wd)


def _kvq_kernel(x_ref, nkv_ref, nq_ref, wkv_ref, wq_ref, cos_ref, sin_ref, *refs, dils):
    q_refs, kv_refs, slab = refs[0:N_GROUPS], refs[N_GROUPS:2 * N_GROUPS], refs[2 * N_GROUPS]
    tm = x_ref.shape[0]
    xu = _rms_unit(x_ref[...])
    ykv = _dot((xu * nkv_ref[...]).astype(BF16), wkv_ref[...])
    yq = _dot((xu * nq_ref[...]).astype(BF16), wq_ref[...])
    cos_t, sin_t = cos_ref[...], sin_ref[...]
    lane = lax.broadcasted_iota(jnp.int32, cos_t.shape, 1)
    low_half = (lane % HEAD_DIM) < (ROT_DIM // 2)

    def rope(yb):
        partner = jnp.where(low_half,
                            pltpu.roll(yb, V7X_LANES - ROT_DIM // 2, axis=1),
                            pltpu.roll(yb, ROT_DIM // 2, axis=1))
        return yb * cos_t + partner * sin_t

    slab_id = 0
    for g, dil in enumerate(dils):
        n = tm // dil
        for half in range(GROUP_COLS // V7X_LANES):
            c0 = g * GROUP_COLS + half * V7X_LANES
            src = slice(c0, c0 + V7X_LANES)
            vsrc = slice(N_ATTN + c0, N_ATTN + c0 + V7X_LANES)
            pieces = ((q_refs[g], half * V7X_LANES, rope(yq[:, src])),
                      (kv_refs[g], half * V7X_LANES, rope(ykv[:, src])),
                      (kv_refs[g], GROUP_COLS + half * V7X_LANES, ykv[:, vsrc]))
            for out_ref, col, val in pieces:
                cols = slice(col, col + V7X_LANES)
                if dil == 1:
                    out_ref[0, :, cols] = val.astype(out_ref.dtype)
                else:
                    slab[slab_id] = val
                    for r in range(dil):
                        out_ref[r, :, cols] = slab[slab_id, pl.ds(r, n, stride=dil), :].astype(out_ref.dtype)
                    slab_id += 1


def kvq_proj(x, batch, seq_len, nkv, nq, wkv, wq, cos_t, sin_t, dils, tm):
    n, d = x.shape
    tps = seq_len // tm
    assert all(tm % dil == 0 and (tm // dil) % BF16_SUBLANES == 0 for dil in dils)
    n_slabs = max(1, 3 * (GROUP_COLS // V7X_LANES) * sum(dil > 1 for dil in dils))
    out_specs, out_shape = [], []
    for cols, dt in ((GROUP_COLS, BF16), (2 * GROUP_COLS, F32)):
        for dil in dils:
            out_specs.append(pl.BlockSpec((None, dil, tm // dil, cols), lambda b, i: (b, 0, i, 0)))
            out_shape.append(jax.ShapeDtypeStruct((batch, dil, seq_len // dil, cols), dt))
    return pl.pallas_call(
        functools.partial(_kvq_kernel, dils=dils),
        grid=(batch, tps),
        in_specs=[pl.BlockSpec((tm, d), lambda b, i: (b * tps + i, 0)), _const_spec((1, d)), _const_spec((1, d)),
                  _const_spec(wkv.shape), _const_spec(wq.shape),
                  pl.BlockSpec((tm, V7X_LANES), lambda b, i: (i, 0)),
                  pl.BlockSpec((tm, V7X_LANES), lambda b, i: (i, 0))],
        out_specs=out_specs,
        out_shape=out_shape,
        scratch_shapes=[pltpu.VMEM((n_slabs, tm, V7X_LANES), F32)],
        compiler_params=_params("parallel", "parallel"),
        name="kvq_proj",
    )(x, nkv, nq, wkv, wq, cos_t, sin_t)


def rope_tables(pos):
    half = ROT_DIM // 2
    inv_freq = ROPE_THETA ** (-jnp.arange(half, dtype=F32) / half)
    ang = pos.astype(F32)[:, None] * inv_freq[None, :]
    cos, sin = jnp.cos(ang), jnp.sin(ang)
    ones, zeros = jnp.ones_like(cos), jnp.zeros_like(sin)
    pad = (HEAD_DIM - ROT_DIM) // half
    cos_h = jnp.concatenate([cos, cos] + [ones] * pad, axis=1)
    sin_h = jnp.concatenate([-sin, sin] + [zeros] * pad, axis=1)
    reps = V7X_LANES // HEAD_DIM
    return jnp.tile(cos_h, (1, reps)), jnp.tile(sin_h, (1, reps))


def _head_masks(shape):
    lane = lax.broadcasted_iota(jnp.int32, shape, 1)
    return [(lane // HEAD_DIM) == h for h in range(HEADS_PER_GROUP)]


def _attn_prompt_kernel(q_ref, kvc_ref, kvp_ref, o_ref, l_ref, *, n_keys):
    i = pl.program_id(2)
    tq = n_keys
    n_sub = q_ref.shape[0] // tq
    kv = jnp.concatenate([kvp_ref[...], kvc_ref[...]], axis=0)
    k = kv[:, 0:GROUP_COLS].astype(BF16)
    v = kv[:, GROUP_COLS:2 * GROUP_COLS].astype(BF16)
    hm_q = _head_masks((tq, GROUP_COLS))
    hm_v = _head_masks((2 * tq, GROUP_COLS))
    qi = lax.broadcasted_iota(jnp.int32, (tq, 2 * tq), 0)
    kj = lax.broadcasted_iota(jnp.int32, (tq, 2 * tq), 1)
    rel = qi + tq - kj
    band = (rel >= 0) & (rel <= n_keys)
    bias = jnp.where(band, 0.0, -jnp.inf)
    bias_start = jnp.where(band & (kj >= tq), 0.0, -jnp.inf)
    bias_all = jnp.concatenate([bias] * HEADS_PER_GROUP, axis=0)
    bias_first = jnp.where(i == 0, jnp.concatenate([bias_start] * HEADS_PER_GROUP, axis=0), bias_all)
    for jb in range(n_sub):
        q = q_ref[jb * tq:(jb + 1) * tq, :]
        kb = k[jb * tq:(jb + 2) * tq]
        vb = v[jb * tq:(jb + 2) * tq]
        qs = jnp.concatenate([jnp.where(m, q, jnp.zeros_like(q)) for m in hm_q], axis=0)
        s = _dot_nt(qs, kb) * (HEAD_DIM ** -0.5) + (bias_first if jb == 0 else bias_all)
        m = jnp.max(s, axis=-1, keepdims=True)
        p = jnp.exp(s - m)
        l = jnp.sum(p, axis=-1, keepdims=True)
        pn = (p / l).astype(BF16)
        lse = m + jnp.log(l)
        o = jnp.zeros((tq, GROUP_COLS), F32)
        lmap = jnp.zeros((tq, GROUP_COLS), F32)
        for h in range(HEADS_PER_GROUP):
            rows = slice(h * tq, (h + 1) * tq)
            o = o + _dot(pn[rows], jnp.where(hm_v[h], vb, jnp.zeros_like(vb)))
            lmap = jnp.where(hm_q[h], lse[rows], lmap)
        o_ref[jb * tq:(jb + 1) * tq, :] = o
        l_ref[jb * tq:(jb + 1) * tq, :] = lmap


def attn_prompt(q, kv, group):
    window, dil = GROUPS[group]
    n_keys = window // dil
    batch, _, L, _ = q.shape
    tqb = _row_tile(L, ATTN_Q_ROWS)
    assert tqb % n_keys == 0
    per = tqb // n_keys
    cur = lambda b, r, i: (b, r, i, 0)
    prev = lambda b, r, i: (b, r, jnp.maximum(i * per - 1, 0), 0)
    return pl.pallas_call(
        functools.partial(_attn_prompt_kernel, n_keys=n_keys),
        grid=(batch, dil, L // tqb),
        in_specs=[pl.BlockSpec((None, None, tqb, GROUP_COLS), cur),
                  pl.BlockSpec((None, None, tqb, 2 * GROUP_COLS), cur),
                  pl.BlockSpec((None, None, n_keys, 2 * GROUP_COLS), prev)],
        out_specs=[pl.BlockSpec((None, None, tqb, GROUP_COLS), cur)] * 2,
        out_shape=[jax.ShapeDtypeStruct((batch, dil, L, GROUP_COLS), F32)] * 2,
        compiler_params=_params("parallel", "parallel", "parallel"),
        name=f"attn_prompt_g{group}",
    )(q, kv, kv)


def _attn_sample_kernel(*refs, n_new, dils):
    q_refs, kvn_refs, c_refs = refs[0:N_GROUPS], refs[N_GROUPS:2 * N_GROUPS], refs[2 * N_GROUPS:3 * N_GROUPS]
    o_ref, l_ref = refs[3 * N_GROUPS:3 * N_GROUPS + 2]
    scale = HEAD_DIM ** -0.5
    for g, dil in enumerate(dils):
        c_ref, kvn_ref = c_refs[g], kvn_refs[g]
        heads = slice(g * HEADS_PER_GROUP, (g + 1) * HEADS_PER_GROUP)
        row_id = lax.broadcasted_iota(jnp.int32, (c_ref.shape[0], HEADS_PER_GROUP, 1), 0)
        for t in range(n_new):
            q = q_refs[g][t]
            if dil == 1:
                k, v, new_rows = c_ref[:, 0], c_ref[:, 1], range(t + 1)
            else:
                k, v, new_rows = c_ref[:, t, 0], c_ref[:, t, 1], (t,)
            s = jnp.sum(k * q[None], axis=-1, keepdims=True) * scale
            if dil == 1:
                s = jnp.where(row_id >= t, s, -jnp.inf)
            s_new = [jnp.sum(kvn_ref[u, 0] * q, axis=-1, keepdims=True) * scale for u in new_rows]
            m = jnp.max(s, axis=0)
            for sn in s_new:
                m = jnp.maximum(m, sn)
            p = jnp.exp(s - m[None])
            l = jnp.sum(p, axis=0)
            acc = jnp.sum(p * v, axis=0)
            for u, sn in zip(new_rows, s_new):
                pn = jnp.exp(sn - m)
                l = l + pn
                acc = acc + pn * kvn_ref[u, 1]
            o_ref[t, heads, :] = acc / l
            l_ref[t, heads, :] = jnp.broadcast_to(m + jnp.log(l), (HEADS_PER_GROUP, HEAD_DIM))


def attn_sample(qs, kv_news, caches, n_new):
    nb = qs[0].shape[0]
    dils = tuple(dil for _, dil in GROUPS)
    tail = qs[0].shape[2:]
    views, c_specs = [], []
    for c, (window, dil) in zip(caches, GROUPS):
        if dil == 1:
            views.append(c)
            c_specs.append(pl.BlockSpec((None, window, 2) + tail, lambda b: (b, 0, 0, 0, 0)))
        else:
            assert dil % n_new == 0
            views.append(c.reshape(nb, window // dil, dil, 2, *tail))
            c_specs.append(pl.BlockSpec((None, window // dil, n_new, 2) + tail, lambda b: (b, 0, 0, 0, 0, 0)))
    n_heads = N_GROUPS * HEADS_PER_GROUP
    out_spec = pl.BlockSpec((None, n_new, n_heads, HEAD_DIM), lambda b: (b, 0, 0, 0))
    return pl.pallas_call(
        functools.partial(_attn_sample_kernel, n_new=n_new, dils=dils),
        grid=(nb,),
        in_specs=([pl.BlockSpec((None, n_new) + tail, lambda b: (b, 0, 0, 0))] * N_GROUPS
                  + [pl.BlockSpec((None, n_new, 2) + tail, lambda b: (b, 0, 0, 0, 0))] * N_GROUPS + c_specs),
        out_specs=[out_spec] * 2,
        out_shape=[jax.ShapeDtypeStruct((nb, n_new, n_heads, HEAD_DIM), F32)] * 2,
        compiler_params=_params("parallel"),
        name="attn_sample",
    )(*qs, *kv_news, *views)


def _combine_wo_kernel(*refs, dils):
    o_refs, l_refs = refs[0:N_GROUPS], refs[N_GROUPS:2 * N_GROUPS]
    x_ref, w_ref, out_ref, slab = refs[2 * N_GROUPS:2 * N_GROUPS + 4]
    tm = x_ref.shape[0]
    slab_id = 0

    def in_position_order(ref, dil):
        nonlocal slab_id
        halves = []
        for half in range(GROUP_COLS // V7X_LANES):
            cols = slice(half * V7X_LANES, (half + 1) * V7X_LANES)
            if dil == 1:
                halves.append(ref[0, :, cols])
            else:
                for r in range(dil):
                    slab[slab_id, pl.ds(r, tm // dil, stride=dil), :] = ref[r, :, cols]
                halves.append(slab[slab_id])
                slab_id += 1
        return jnp.concatenate(halves, axis=1)

    ls = [in_position_order(l_refs[g], dils[g]) for g in range(N_GROUPS)]
    mx = jnp.maximum(jnp.maximum(ls[0], ls[1]), ls[2])
    es = [jnp.exp(l - mx) for l in ls]
    tot = es[0] + es[1] + es[2]
    acc = x_ref[...]
    for g in range(N_GROUPS):
        og = (in_position_order(o_refs[g], dils[g]) * (es[g] / tot)).astype(BF16)
        acc = acc + _dot(og, w_ref[g * GROUP_COLS:(g + 1) * GROUP_COLS, :])
    out_ref[...] = acc


def combine_wo(os_, ls, x, batch, seq_len, w_o, dils, tm):
    n, d = x.shape
    tps = seq_len // tm
    gspecs = [pl.BlockSpec((None, dil, tm // dil, GROUP_COLS), lambda b, i: (b, 0, i, 0)) for dil in dils]
    n_slabs = max(1, 2 * (GROUP_COLS // V7X_LANES) * sum(dil > 1 for dil in dils))
    return pl.pallas_call(
        functools.partial(_combine_wo_kernel, dils=dils),
        grid=(batch, tps),
        in_specs=gspecs + gspecs + [pl.BlockSpec((tm, d), lambda b, i: (b * tps + i, 0)), _const_spec(w_o.shape)],
        out_specs=pl.BlockSpec((tm, d), lambda b, i: (b * tps + i, 0)),
        out_shape=jax.ShapeDtypeStruct((n, d), F32),
        scratch_shapes=[pltpu.VMEM((n_slabs, tm, V7X_LANES), F32)],
        compiler_params=_params("parallel", "parallel"),
        name="combine_wo",
    )(*os_, *ls, x, w_o)


def _router_kernel(x_ref, nw_ref, wr_ref, xn_ref, meta_ref, metat_ref, cnt_ref, *, n_experts):
    xn = _rms(x_ref[...], nw_ref[...])
    xn_ref[...] = xn.astype(BF16)
    xh = xn.astype(BF16)
    xl = (xn - xh.astype(F32)).astype(BF16)
    wr = wr_ref[...]
    wh = wr.astype(BF16)
    wl = (wr - wh.astype(F32)).astype(BF16)
    logits = _dot(xh, wh) + (_dot(xl, wh) + _dot(xh, wl))
    lane = lax.broadcasted_iota(jnp.int32, logits.shape, 1)
    logits = jnp.where(lane < n_experts, logits, -jnp.inf)
    v1 = jnp.max(logits, axis=-1, keepdims=True)
    i1 = jnp.min(jnp.where(logits == v1, lane, V7X_LANES), axis=-1, keepdims=True)
    rest = jnp.where(lane == i1, -jnp.inf, logits)
    v2 = jnp.max(rest, axis=-1, keepdims=True)
    i2 = jnp.min(jnp.where(rest == v2, lane, V7X_LANES), axis=-1, keepdims=True)
    e2 = jnp.exp(v2 - v1)
    den = 1.0 + e2
    g1, g2 = 1.0 / den, e2 / den
    onehot = ((lane == i1) | (lane == i2)).astype(BF16)
    tm = onehot.shape[0]
    tri = (lax.broadcasted_iota(jnp.int32, (tm, tm), 0) >= lax.broadcasted_iota(jnp.int32, (tm, tm), 1)).astype(BF16)
    csum = _dot(tri, onehot)
    r1 = jnp.sum(jnp.where(lane == i1, csum, 0.0), axis=-1, keepdims=True) - 1.0
    r2 = jnp.sum(jnp.where(lane == i2, csum, 0.0), axis=-1, keepdims=True) - 1.0
    cnt_ref[...] = csum[tm - 1:tm, :].astype(jnp.int32)
    cols = (i1.astype(F32), i2.astype(F32), r1, r2, g1, g2)
    meta = jnp.zeros(logits.shape, F32)
    for ci, col in enumerate(cols):
        meta = jnp.where(lane == ci, col, meta)
    meta_ref[...] = meta[:, :META_COLS]
    metat_ref[...] = meta.T[:META_COLS, :]


def router(x, nw, w_router, tm):
    n, d = x.shape
    n_experts = w_router.shape[1]
    wr = jnp.pad(w_router, ((0, 0), (0, V7X_LANES - n_experts)))
    return pl.pallas_call(
        functools.partial(_router_kernel, n_experts=n_experts),
        grid=(n // tm,),
        in_specs=[pl.BlockSpec((tm, d), lambda i: (i, 0)), _const_spec((1, d)), _const_spec((d, V7X_LANES))],
        out_specs=[pl.BlockSpec((tm, d), lambda i: (i, 0)),
                   pl.BlockSpec((tm, META_COLS), lambda i: (i, 0)),
                   pl.BlockSpec((None, META_COLS, tm), lambda i: (i, 0, 0)),
                   pl.BlockSpec((None, 1, V7X_LANES), lambda i: (i, 0, 0))],
        out_shape=[jax.ShapeDtypeStruct((n, d), BF16),
                   jax.ShapeDtypeStruct((n, META_COLS), F32),
                   jax.ShapeDtypeStruct((n // tm, META_COLS, tm), F32),
                   jax.ShapeDtypeStruct((n // tm, 1, V7X_LANES), jnp.int32)],
        compiler_params=_params("parallel"),
        name="router",
    )(x, nw, wr)


def _segment_bits(tm):
    bits, b = [], SEG_ALIGN
    while b <= tm:
        bits.append(b)
        b *= 2
    return bits[::-1]


def _compact_rows(tm, n_experts):
    worst = TOP_K * tm + n_experts * (SEG_ALIGN - 1)
    return _round_up(worst, BF16_SUBLANES)


def _segment_copies(seg_sm, base_sm, tile, n_experts, tm, hbm_ref, vmem_ref, sems, to_hbm):
    out = []
    local = 0
    for e in range(n_experts):
        n = seg_sm[tile * n_experts + e]
        base = base_sm[tile * n_experts + e]
        for b, bit in enumerate(_segment_bits(tm)):
            done = n & ~(2 * bit - 1)
            src = vmem_ref.at[pl.ds(pl.multiple_of(local + done, SEG_ALIGN), bit)]
            dst = hbm_ref.at[pl.ds(pl.multiple_of(base + done, SEG_ALIGN), bit)]
            if not to_hbm:
                src, dst = dst, src
            out.append(((n & bit) != 0, pltpu.make_async_copy(src, dst, sems.at[e, b])))
        local = local + n
    return out, local


def _token_dest(e_k, r_k, seg_sm, tile, n_experts):
    dest = r_k
    local = 0
    for e in range(n_experts):
        dest = dest + jnp.where(e_k == e, local, 0)
        local = local + seg_sm[tile * n_experts + e]
    return dest


def _start_all(copies):
    for cond, cp in copies:
        pl.when(cond)(cp.start)


def _wait_all(copies):
    for cond, cp in copies:
        pl.when(cond)(cp.wait)


def _zero_copies(zn_sm, zbase_sm, n_spans, zero_rows, xs_hbm, zero_ref, zero_sems):
    out = []
    for k in range(n_spans):
        n, base = zn_sm[k], zbase_sm[k]
        for b, bit in enumerate(_segment_bits(zero_rows)):
            done = n & ~(2 * bit - 1)
            dst = xs_hbm.at[pl.ds(pl.multiple_of(base + done, SEG_ALIGN), bit)]
            out.append(((n & bit) != 0, pltpu.make_async_copy(zero_ref.at[pl.ds(0, bit)], dst, zero_sems.at[k, b])))
    return out


def _dispatch_kernel(seg_sm, base_sm, zn_sm, zbase_sm, tail_sm, xn_ref, metat_ref, *refs,
                     n_experts, n_tiles, tile0, n_spans, tmg, first_call):
    xs_hbm, comp_ref, zero_ref, sems, zero_sems, tail_sems = refs[-6:]
    i = pl.program_id(0)
    tile, slot = tile0 + i, i % 2
    tm, d = xn_ref.shape
    rows = comp_ref.shape[1]
    zero_rows = zero_ref.shape[0]

    def copies(t, s):
        return _segment_copies(seg_sm, base_sm, t, n_experts, tm, xs_hbm, comp_ref.at[s], sems.at[s], to_hbm=True)[0]

    if first_call:
        zeros = _zero_copies(zn_sm, zbase_sm, n_spans, zero_rows, xs_hbm, zero_ref, zero_sems)

        @pl.when(i == 0)
        def _():
            zero_ref[...] = jnp.zeros_like(zero_ref)
            _start_all(zeros)

    @pl.when(i < n_tiles)
    def _():
        mt = metat_ref[...]
        as_int = lambda v: v.astype(jnp.int32)
        dest1 = _token_dest(as_int(mt[0:1, :]), as_int(mt[2:3, :]), seg_sm, tile, n_experts)
        dest2 = _token_dest(as_int(mt[1:2, :]), as_int(mt[3:4, :]), seg_sm, tile, n_experts)
        row_id = lax.broadcasted_iota(jnp.int32, (rows, tm), 0)
        p1, p2 = row_id == dest1, row_id == dest2
        comp_ref[slot, :, 0:d] = _dot((p1 | p2).astype(BF16), xn_ref[...])
        gate = jnp.sum(jnp.where(p1, mt[4:5, :], 0.0) + jnp.where(p2, mt[5:6, :], 0.0), axis=-1, keepdims=True)
        comp_ref[slot, :, d:d + V7X_LANES] = jnp.broadcast_to(gate, (rows, V7X_LANES))
        _start_all(copies(tile, slot))

    @pl.when((i > 0) & (i <= n_tiles))
    def _():
        _wait_all(copies(tile - 1, 1 - slot))

    if first_call:
        @pl.when(i == n_tiles)
        def _():
            _wait_all(zeros)

        @pl.when((i > n_tiles) & (i - n_tiles - 1 < tail_sm[1]))
        def _():
            start = tail_sm[0] + (i - n_tiles - 1) * tmg
            parts = [pltpu.make_async_copy(
                zero_ref, xs_hbm.at[pl.ds(pl.multiple_of(start + p * zero_rows, SEG_ALIGN), zero_rows)], tail_sems.at[p])
                for p in range(tmg // zero_rows)]
            for cp in parts:
                cp.start()
            for cp in parts:
                cp.wait()


def dispatch(xn, metat, tables, n_experts, total_rows, tail_tiles, tm, tmg, tile0, xs_prev):
    n, d = xn.shape
    n_tiles = n // tm
    n_spans = tables[2].shape[0]
    rows = _compact_rows(tm, n_experts)
    zero_rows = tmg // 2
    first_call = xs_prev is None
    last_tile = lambda i: jnp.minimum(i, n_tiles - 1)
    in_specs = [pl.BlockSpec((tm, d), lambda i, *_: (last_tile(i), 0)),
                pl.BlockSpec((None, META_COLS, tm), lambda i, *_: (last_tile(i), 0, 0))]
    args = [*tables, xn, metat]
    if not first_call:
        in_specs.append(pl.BlockSpec(memory_space=pl.ANY))
        args.append(xs_prev)
    grid_spec = pltpu.PrefetchScalarGridSpec(
        num_scalar_prefetch=len(tables),
        grid=(n_tiles + 1 + (tail_tiles if first_call else 0),),
        in_specs=in_specs,
        out_specs=pl.BlockSpec(memory_space=pl.ANY),
        scratch_shapes=[pltpu.VMEM((2, rows, d + V7X_LANES), F32),
                        pltpu.VMEM((zero_rows, d + V7X_LANES), F32),
                        pltpu.SemaphoreType.DMA((2, n_experts, len(_segment_bits(tm)))),
                        pltpu.SemaphoreType.DMA((n_spans, len(_segment_bits(zero_rows)))),
                        pltpu.SemaphoreType.DMA((tmg // zero_rows,))],
    )
    return pl.pallas_call(
        functools.partial(_dispatch_kernel, n_experts=n_experts, n_tiles=n_tiles, tile0=tile0, n_spans=n_spans,
                          tmg=tmg, first_call=first_call),
        grid_spec=grid_spec,
        out_shape=jax.ShapeDtypeStruct((total_rows, d + V7X_LANES), F32),
        input_output_aliases={} if first_call else {len(args) - 1: 0},
        compiler_params=_params("arbitrary"),
        name="moe_dispatch",
    )(*args)


def _expert_ffn_kernel(te_sm, nu_sm, xs_ref, wg_ref, wu_ref, wd_ref, ys_ref, xb_ref, acc_ref):
    del te_sm
    j, c = pl.program_id(0), pl.program_id(1)
    last_c = pl.num_programs(1) - 1
    d = xb_ref.shape[1]
    used = j < nu_sm[0]

    @pl.when(used & (c == 0))
    def _():
        xb_ref[...] = xs_ref[:, 0:d].astype(BF16)
        acc_ref[...] = jnp.zeros_like(acc_ref)

    @pl.when(used)
    def _():
        xb = xb_ref[...]
        h = _silu(_dot(xb, wg_ref[...])) * _dot(xb, wu_ref[...])
        acc_ref[...] += _dot(h.astype(BF16), wd_ref[...])

    @pl.when(used & (c == last_c))
    def _():
        ys_ref[...] = xs_ref[:, d:d + 1] * acc_ref[...]

    @pl.when(jnp.logical_not(used) & (c == last_c))
    def _():
        ys_ref[...] = jnp.zeros_like(ys_ref)


def expert_ffn(xs, tile_expert, n_used, wg, wu, wd, tmg):
    total_rows, dx = xs.shape
    n_experts, d, de = wg.shape
    tf = next(c for c in (MOE_FF_CHUNK, MOE_FF_CHUNK // 2, _ff_chunk(de)) if de % c == 0)
    last_used = lambda j, nu: jnp.minimum(j, nu[0] - 1)
    grid_spec = pltpu.PrefetchScalarGridSpec(
        num_scalar_prefetch=2,
        grid=(total_rows // tmg, de // tf),
        in_specs=[pl.BlockSpec((tmg, dx), lambda j, c, te, nu: (last_used(j, nu), 0)),
                  pl.BlockSpec((None, d, tf), lambda j, c, te, nu: (te[j], 0, c)),
                  pl.BlockSpec((None, d, tf), lambda j, c, te, nu: (te[j], 0, c)),
                  pl.BlockSpec((None, tf, d), lambda j, c, te, nu: (te[j], c, 0))],
        out_specs=pl.BlockSpec((tmg, d), lambda j, c, te, nu: (j, 0)),
        scratch_shapes=[pltpu.VMEM((tmg, d), BF16), pltpu.VMEM((tmg, d), F32)],
    )
    return pl.pallas_call(
        _expert_ffn_kernel,
        grid_spec=grid_spec,
        out_shape=jax.ShapeDtypeStruct((total_rows, d), F32),
        compiler_params=_params("arbitrary", "arbitrary"),
        name="moe_expert_ffn",
    )(tile_expert, n_used, xs, wg, wu, wd)


def _combine_kernel(seg_sm, base_sm, x_ref, meta_ref, nf_ref, ys_hbm, y_ref, comp_ref, sems, *, n_experts, tile0):
    i, n_steps = pl.program_id(0), pl.num_programs(0)
    tile, slot = tile0 + i, i % 2
    tm, d = x_ref.shape
    rows = comp_ref.shape[1]

    def copies(t, s):
        return _segment_copies(seg_sm, base_sm, t, n_experts, tm, ys_hbm, comp_ref.at[s], sems.at[s], to_hbm=False)

    @pl.when(i == 0)
    def _():
        _start_all(copies(tile, slot)[0])

    @pl.when(i + 1 < n_steps)
    def _():
        _start_all(copies(tile + 1, 1 - slot)[0])

    meta = meta_ref[...]
    as_int = lambda v: v.astype(jnp.int32)
    dest1 = _token_dest(as_int(meta[:, 0:1]), as_int(meta[:, 2:3]), seg_sm, tile, n_experts)
    dest2 = _token_dest(as_int(meta[:, 1:2]), as_int(meta[:, 3:4]), seg_sm, tile, n_experts)
    col_id = lax.broadcasted_iota(jnp.int32, (tm, rows), 1)
    pick = ((col_id == dest1) | (col_id == dest2)).astype(BF16)
    mine, n_rows = copies(tile, slot)
    _wait_all(mine)
    live = lax.broadcasted_iota(jnp.int32, (rows, 1), 0) < n_rows
    ys = jnp.where(live, comp_ref[slot], 0.0)
    hi = ys.astype(BF16)
    lo = (ys - hi.astype(F32)).astype(BF16)
    y_ref[...] = _rms(x_ref[...] + (_dot(pick, hi) + _dot(pick, lo)), nf_ref[...])


def combine_final(x, meta, nf, ys, seg, base, n_experts, tm, tile0):
    n, d = x.shape
    rows = _compact_rows(tm, n_experts)
    grid_spec = pltpu.PrefetchScalarGridSpec(
        num_scalar_prefetch=2,
        grid=(n // tm,),
        in_specs=[pl.BlockSpec((tm, d), lambda i, *_: (i, 0)),
                  pl.BlockSpec((tm, META_COLS), lambda i, *_: (i, 0)),
                  pl.BlockSpec((1, d), lambda i, *_: (0, 0)),
                  pl.BlockSpec(memory_space=pl.ANY)],
        out_specs=pl.BlockSpec((tm, d), lambda i, *_: (i, 0)),
        scratch_shapes=[pltpu.VMEM((2, rows, d), F32),
                        pltpu.SemaphoreType.DMA((2, n_experts, len(_segment_bits(tm))))],
    )
    return pl.pallas_call(
        functools.partial(_combine_kernel, n_experts=n_experts, tile0=tile0),
        grid_spec=grid_spec,
        out_shape=jax.ShapeDtypeStruct((n, d), F32),
        compiler_params=_params("arbitrary"),
        name="moe_combine",
    )(seg, base, x, meta, nf, ys)


def moe_final(xs, tms, nw, nf, w_router, wg, wu, wd, tmg):
    n_experts = w_router.shape[1]
    routed = [router(x, nw, w_router, tm) for x, tm in zip(xs, tms)]
    n_tiles = [x.shape[0] // tm for x, tm in zip(xs, tms)]
    n_tokens = sum(x.shape[0] for x in xs)
    cnt = jnp.concatenate([r[3][:, 0, :n_experts] for r in routed], axis=0)
    seg = _round_up(cnt, SEG_ALIGN)
    within = jnp.cumsum(seg, axis=0) - seg
    exp_live = jnp.sum(seg, axis=0)
    exp_rows = _round_up(exp_live, tmg)
    exp_end = jnp.cumsum(exp_rows)
    base = (exp_end - exp_rows)[None, :] + within
    total_rows = _round_up(TOP_K * n_tokens + sum(n_tiles) * n_experts * (SEG_ALIGN - 1)
                           + n_experts * (tmg - SEG_ALIGN), tmg)
    n_used = (exp_end[-1] // tmg).reshape(1).astype(jnp.int32)
    tile_start = jnp.arange(total_rows // tmg, dtype=jnp.int32) * tmg
    tile_expert = jnp.minimum(jnp.sum(tile_start[:, None] >= exp_end[None, :], axis=1), n_experts - 1).astype(jnp.int32)
    i32 = lambda a: a.reshape(-1).astype(jnp.int32)
    zero_n = jnp.concatenate([exp_rows - exp_live, seg[n_tiles[0]:].reshape(-1)])
    zero_base = jnp.concatenate([exp_end - exp_rows + exp_live, base[n_tiles[0]:].reshape(-1)])
    tail = jnp.stack([exp_end[-1], (total_rows - exp_end[-1]) // tmg])
    tail_tiles = (total_rows - TOP_K * n_tokens) // tmg
    tables = (i32(seg), i32(base), i32(zero_n), i32(zero_base), i32(tail))
    rows_buf, tile0 = None, 0
    for (xn, _, metat, _), x, tm, nt in zip(routed, xs, tms, n_tiles):
        rows_buf = dispatch(xn, metat, tables, n_experts, total_rows, tail_tiles, tm, tmg, tile0, rows_buf)
        tile0 += nt
    ys = expert_ffn(rows_buf, tile_expert, n_used, wg, wu, wd, tmg)
    outs, tile0 = [], 0
    for (_, meta, _, _), x, tm, nt in zip(routed, xs, tms, n_tiles):
        outs.append(combine_final(x, meta, nf, ys, tables[0], tables[1], n_experts, tm, tile0))
        tile0 += nt
    return outs


def kernel(x_prompt, x_sample, state_conv, cache_kv_g0, cache_kv_g1, cache_kv_g2, norm_mix, norm_ffn, norm_kv, norm_final, w_pw1, b_pw1, w_dw, b_dw, ln_conv_g, ln_conv_b, w_pw2, b_pw2, w_q, w_kv, w_o, w_gate_dense, w_up_dense, w_down_dense, w_router, w_gate_exp, w_up_exp, w_down_exp):
    batch, seq_len, d = x_prompt.shape
    dec_batch, dec_seq, _ = x_sample.shape
    caches = (cache_kv_g0, cache_kv_g1, cache_kv_g2)
    assert norm_mix.shape[0] == 2 and state_conv.shape[0] == 1 and w_q.shape[0] == 1
    assert all(c.shape[1] == w for c, (w, _) in zip(caches, GROUPS))
    assert dec_seq <= SAMPLE_Q_ROWS

    row = lambda v: v.reshape(1, -1).astype(F32)
    bf = lambda w: w.astype(BF16)
    n_p = batch * seq_len
    n_s = dec_batch * dec_seq
    tm_p = _row_tile(seq_len, 512)
    tm_s = n_s
    dils = tuple(dil for _, dil in GROUPS)
    no_dil = (1,) * N_GROUPS

    wdw = jnp.pad(w_dw[0], ((0, TAP_ROWS - CONV_WIDTH), (0, 0)))
    conv_w = (wdw, row(b_dw[0]), row(ln_conv_g[0]), row(ln_conv_b[0]), bf(w_pw2[0]), row(b_pw2[0]))
    w_pw1_b, b_pw1_r = bf(w_pw1[0]), row(b_pw1[0])
    ffn_w = (bf(w_gate_dense[0]), bf(w_up_dense[0]), bf(w_down_dense[0]))
    kvq_w = (row(norm_kv), row(norm_mix[1]), bf(w_kv), bf(w_q[0]))
    w_o_b = bf(w_o[0])
    moe_w = (bf(w_gate_exp[0]), bf(w_up_exp[0]), bf(w_down_exp[0]))

    cos_p, sin_p = rope_tables(jnp.arange(seq_len))
    cos_s, sin_s = rope_tables(PAST_LEN + jnp.arange(n_s) // dec_batch)

    hp = x_prompt.reshape(n_p, d)
    hs = x_sample.transpose(1, 0, 2).reshape(n_s, d)
    g_p = pw1_glu(hp, row(norm_mix[0]), w_pw1_b, b_pw1_r, tm_p)
    g_s = pw1_glu(hs, row(norm_mix[0]), w_pw1_b, b_pw1_r, tm_s)
    wdw_rep = jnp.broadcast_to(w_dw[0][:, None, :], (CONV_WIDTH, V7X_SUBLANES, d))
    hp = conv_prompt(g_p, hp, seq_len, wdw_rep, *conv_w[1:], tm_p)
    full_s = jnp.concatenate([state_conv[0].transpose(1, 0, 2), g_s.reshape(dec_seq, dec_batch, d)], axis=0)
    hs = conv_sample(full_s, hs, *conv_w)
    conv_prompt_out = g_p.reshape(batch, seq_len, d)[:, seq_len - (CONV_WIDTH - 1):][None]
    conv_sample_out = full_s[dec_seq:].transpose(1, 0, 2)[None]

    hp = ffn(hp, row(norm_ffn[0]), *ffn_w, tm_p)
    hs = ffn(hs, row(norm_ffn[0]), *ffn_w, tm_s)

    kvq_p = kvq_proj(hp, batch, seq_len, *kvq_w, cos_p, sin_p, dils, tm_p)
    q_p, kv_p = kvq_p[0:N_GROUPS], kvq_p[N_GROUPS:]
    kvq_s = kvq_proj(hs, 1, n_s, *kvq_w, cos_s, sin_s, no_dil, tm_s)
    to_bm = lambda a: a.reshape(dec_seq, dec_batch, -1).transpose(1, 0, 2)
    q_s = [to_bm(a).astype(F32).reshape(dec_batch, dec_seq, HEADS_PER_GROUP, HEAD_DIM) for a in kvq_s[0:N_GROUPS]]
    kv_s = [to_bm(a).reshape(dec_batch, dec_seq, 2, HEADS_PER_GROUP, HEAD_DIM) for a in kvq_s[N_GROUPS:]]

    att_p = [attn_prompt(q_p[g], kv_p[g], g) for g in range(N_GROUPS)]
    hp = combine_wo([a[0] for a in att_p], [a[1] for a in att_p], hp, batch, seq_len, w_o_b, dils, tm_p)

    hs = to_bm(hs).reshape(n_s, d)
    o_s, l_s = attn_sample(q_s, kv_s, caches, dec_seq)
    group_rows = lambda a, g: a[:, :, g * HEADS_PER_GROUP:(g + 1) * HEADS_PER_GROUP].reshape(1, 1, n_s, GROUP_COLS)
    hs = combine_wo([group_rows(o_s, g) for g in range(N_GROUPS)], [group_rows(l_s, g) for g in range(N_GROUPS)],
                    hs, 1, n_s, w_o_b, no_dil, tm_s)

    y_p, y_s = moe_final([hp, hs], [tm_p, tm_s], row(norm_ffn[1]), row(norm_final), w_router[0], *moe_w,
                         MOE_ROW_TILE)

    kv_out = []
    for g, (window, dil) in enumerate(GROUPS):
        keep = min(window, seq_len)
        tail = kv_p[g][:, :, (seq_len - keep) // dil:, :]
        kv_out.append(tail.transpose(0, 2, 1, 3).reshape(batch, keep, 2, HEADS_PER_GROUP, HEAD_DIM))
        kv_out.append(kv_s[g])
    return (y_p.reshape(batch, seq_len, d), y_s.reshape(dec_batch, dec_seq, d),
            conv_prompt_out, conv_sample_out, *kv_out)
```

```python
import functools

import jax
import jax.numpy as jnp
from jax import lax
from jax.experimental import pallas as pl
from jax.experimental.pallas import tpu as pltpu

F32 = jnp.float32
BF16 = jnp.bfloat16

EPS = 1e-5
CONV_WIDTH = 31
HEAD_DIM = 64
ROT_DIM = HEAD_DIM // 4
HEADS_PER_GROUP = 4
GROUP_COLS = HEADS_PER_GROUP * HEAD_DIM
GROUPS = ((128, 1), (512, 4), (2048, 16))
N_GROUPS = len(GROUPS)
N_ATTN = N_GROUPS * GROUP_COLS
ROPE_THETA = 500000.0
TOP_K = 2
PAST_LEN = 16384

V7X_LANES = 128
V7X_SUBLANES = 8
V7X_VMEM_BYTES = 64 * 2**20
VMEM_LIMIT_BYTES = V7X_VMEM_BYTES - 8 * 2**20
BF16_SUBLANES = 2 * V7X_SUBLANES

HALO_ROWS = 32
TAP_ROWS = 32
CONV_ROW_CHUNK = 32
ATTN_Q_ROWS = 512
SAMPLE_Q_ROWS = 8
META_COLS = 8
SEG_ALIGN = V7X_SUBLANES
MOE_ROW_TILE = 512
MOE_FF_CHUNK = 1792


def _params(*sem):
    return pltpu.CompilerParams(dimension_semantics=sem, vmem_limit_bytes=VMEM_LIMIT_BYTES)


def _rms_unit(x):
    return x * lax.rsqrt(jnp.mean(x * x, axis=-1, keepdims=True) + EPS)


def _rms(x, w):
    return _rms_unit(x) * w


def _silu(x):
    return x * jax.nn.sigmoid(x)


def _dot(a, b):
    return jnp.dot(a, b, preferred_element_type=F32)


def _dot_nt(a, b):
    return lax.dot_general(a, b, (((1,), (1,)), ((), ())), preferred_element_type=F32)


def _row_tile(n_rows, want):
    t = min(n_rows, want)
    assert n_rows % t == 0 and (t % V7X_SUBLANES == 0 or t == n_rows)
    return t


def _round_up(v, m):
    return (v + m - 1) // m * m


def _const_spec(shape):
    return pl.BlockSpec(shape, lambda *_: (0,) * len(shape), pipeline_mode=pl.Buffered(1))


def _pw1_glu_kernel(x_ref, nw_ref, w_ref, b_ref, g_ref):
    d = x_ref.shape[1]
    xn = _rms(x_ref[...], nw_ref[...]).astype(BF16)
    u = _dot(xn, w_ref[...]) + b_ref[...]
    g_ref[...] = u[:, :d] * jax.nn.sigmoid(u[:, d:])


def pw1_glu(x, nw, w, b, tm):
    n, d = x.shape
    return pl.pallas_call(
        _pw1_glu_kernel,
        grid=(n // tm,),
        in_specs=[pl.BlockSpec((tm, d), lambda i: (i, 0)),
                  _const_spec((1, d)), _const_spec((d, 2 * d)), _const_spec((1, 2 * d))],
        out_specs=pl.BlockSpec((tm, d), lambda i: (i, 0)),
        out_shape=jax.ShapeDtypeStruct((n, d), F32),
        compiler_params=_params("parallel"),
        name="pw1_glu",
    )(x, nw, w, b)


def _conv_tail(c, x, lng, lnb, w2, b2):
    mu = jnp.mean(c, axis=-1, keepdims=True)
    cc = c - mu
    y = cc * lax.rsqrt(jnp.mean(cc * cc, axis=-1, keepdims=True) + EPS)
    y = _silu(y * lng + lnb)
    return x + _dot(y.astype(BF16), w2) + b2


def _conv_prompt_kernel(g_ref, halo_ref, x_ref, wdw_ref, bdw_ref, lng_ref, lnb_ref, w2_ref, b2_ref,
                        o_ref, win_ref, c_ref, *, tiles_per_seq):
    tm, d = g_ref.shape
    first = (pl.program_id(0) % tiles_per_seq) == 0
    win_ref[0:HALO_ROWS, :] = jnp.where(first, 0.0, halo_ref[...])
    win_ref[HALO_ROWS:HALO_ROWS + tm, :] = g_ref[...]
    lead = HALO_ROWS - (CONV_WIDTH - 1)
    span = HALO_ROWS + CONV_ROW_CHUNK

    def chunk(ci, carry):
        r0 = pl.multiple_of(ci * CONV_ROW_CHUNK, CONV_ROW_CHUNK)
        w = win_ref[pl.ds(r0, span), :]
        acc = None
        for phase in range(V7X_SUBLANES):
            ws = w if phase == 0 else pltpu.roll(w, span - phase, axis=0)
            for off in range(phase, lead + CONV_WIDTH, V7X_SUBLANES):
                j = off - lead
                if j >= 0:
                    base = off - phase
                    rows = ws[base:base + CONV_ROW_CHUNK, :].reshape(CONV_ROW_CHUNK // V7X_SUBLANES, V7X_SUBLANES, d)
                    term = wdw_ref[j] * rows
                    acc = term if acc is None else acc + term
        c_ref[pl.ds(r0, CONV_ROW_CHUNK), :] = acc.reshape(CONV_ROW_CHUNK, d) + bdw_ref[...]
        return carry

    lax.fori_loop(0, tm // CONV_ROW_CHUNK, chunk, 0)
    o_ref[...] = _conv_tail(c_ref[...], x_ref[...], lng_ref[...], lnb_ref[...], w2_ref[...], b2_ref[...])


def conv_prompt(g, x, seq_len, wdw, bdw, lng, lnb, w2, b2, tm):
    n, d = g.shape
    tiles_per_seq = seq_len // tm
    halo_per_tile = tm // HALO_ROWS
    return pl.pallas_call(
        functools.partial(_conv_prompt_kernel, tiles_per_seq=tiles_per_seq),
        grid=(n // tm,),
        in_specs=[pl.BlockSpec((tm, d), lambda i: (i, 0)),
                  pl.BlockSpec((HALO_ROWS, d), lambda i: (jnp.maximum(i * halo_per_tile - 1, 0), 0)),
                  pl.BlockSpec((tm, d), lambda i: (i, 0)),
                  _const_spec(wdw.shape), _const_spec((1, d)), _const_spec((1, d)), _const_spec((1, d)),
                  _const_spec((d, d)), _const_spec((1, d))],
        out_specs=pl.BlockSpec((tm, d), lambda i: (i, 0)),
        out_shape=jax.ShapeDtypeStruct((n, d), F32),
        scratch_shapes=[pltpu.VMEM((HALO_ROWS + tm, d), F32), pltpu.VMEM((tm, d), F32)],
        compiler_params=_params("parallel"),
        name="conv_prompt",
    )(g, g, x, wdw, bdw, lng, lnb, w2, b2)


def _conv_sample_kernel(full_ref, x_ref, wdw_ref, bdw_ref, lng_ref, lnb_ref, w2_ref, b2_ref, o_ref, *, n_t):
    nb, d = full_ref.shape[1], full_ref.shape[2]
    for t in range(n_t):
        acc = jnp.broadcast_to(bdw_ref[...], (nb, d))
        for j in range(CONV_WIDTH):
            acc = acc + wdw_ref[j:j + 1, :] * full_ref[t + j]
        rows = slice(t * nb, (t + 1) * nb)
        o_ref[rows, :] = _conv_tail(acc, x_ref[rows, :], lng_ref[...], lnb_ref[...], w2_ref[...], b2_ref[...])


def conv_sample(full_tm, x, wdw, bdw, lng, lnb, w2, b2):
    rows_full, nb, d = full_tm.shape
    n_t = rows_full - (CONV_WIDTH - 1)
    n = n_t * nb
    return pl.pallas_call(
        functools.partial(_conv_sample_kernel, n_t=n_t),
        grid=(1,),
        in_specs=[_const_spec((rows_full, nb, d)), _const_spec((n, d)),
                  _const_spec((TAP_ROWS, d)), _const_spec((1, d)), _const_spec((1, d)), _const_spec((1, d)),
                  _const_spec((d, d)), _const_spec((1, d))],
        out_specs=pl.BlockSpec((n, d), lambda i: (0, 0)),
        out_shape=jax.ShapeDtypeStruct((n, d), F32),
        compiler_params=_params("arbitrary"),
        name="conv_sample",
    )(full_tm, x, wdw, bdw, lng, lnb, w2, b2)


def _ffn_kernel(x_ref, nw_ref, wg_ref, wu_ref, wd_ref, o_ref, h_ref, *, ff_chunk):
    ff = wg_ref.shape[1]
    x = x_ref[...]
    xn = _rms(x, nw_ref[...]).astype(BF16)
    for c in range(ff // ff_chunk):
        cols = slice(c * ff_chunk, (c + 1) * ff_chunk)
        h = _silu(_dot(xn, wg_ref[:, cols])) * _dot(xn, wu_ref[:, cols])
        h_ref[:, cols] = h.astype(BF16)
    o_ref[...] = x + _dot(h_ref[...], wd_ref[...])


def _ff_chunk(ff):
    for c in (512, 256, 128):
        if ff % c == 0:
            return c
    return ff


def ffn(x, nw, wg, wu, wd, tm):
    n, d = x.shape
    ff = wg.shape[1]
    return pl.pallas_call(
        functools.partial(_ffn_kernel, ff_chunk=_ff_chunk(ff)),
        grid=(n // tm,),
        in_specs=[pl.BlockSpec((tm, d), lambda i: (i, 0)), _const_spec((1, d)),
                  _const_spec((d, ff)), _const_spec((d, ff)), _const_spec((ff, d))],
        out_specs=pl.BlockSpec((tm, d), lambda i: (i, 0)),
        out_shape=jax.ShapeDtypeStruct((n, d), F32),
        scratch_shapes=[pltpu.VMEM((tm, ff), BF16)],
        compiler_params=_params("parallel"),
        name="ffn_dense",
    )(x, nw, wg, wu, wd)


def _kvq_kernel(x_ref, nkv_ref, nq_ref, wkv_ref, wq_ref, cos_ref, sin_ref, *refs, dils):
    q_refs, kv_refs, slab = refs[0:N_GROUPS], refs[N_GROUPS:2 * N_GROUPS], refs[2 * N_GROUPS]
    tm = x_ref.shape[0]
    xu = _rms_unit(x_ref[...])
    ykv = _dot((xu * nkv_ref[...]).astype(BF16), wkv_ref[...])
    yq = _dot((xu * nq_ref[...]).astype(BF16), wq_ref[...])
    cos_t, sin_t = cos_ref[...], sin_ref[...]
    lane = lax.broadcasted_iota(jnp.int32, cos_t.shape, 1)
    low_half = (lane % HEAD_DIM) < (ROT_DIM // 2)

    def rope(yb):
        partner = jnp.where(low_half,
                            pltpu.roll(yb, V7X_LANES - ROT_DIM // 2, axis=1),
                            pltpu.roll(yb, ROT_DIM // 2, axis=1))
        return yb * cos_t + partner * sin_t

    slab_id = 0
    for g, dil in enumerate(dils):
        n = tm // dil
        for half in range(GROUP_COLS // V7X_LANES):
            c0 = g * GROUP_COLS + half * V7X_LANES
            src = slice(c0, c0 + V7X_LANES)
            vsrc = slice(N_ATTN + c0, N_ATTN + c0 + V7X_LANES)
            pieces = ((q_refs[g], half * V7X_LANES, rope(yq[:, src])),
                      (kv_refs[g], half * V7X_LANES, rope(ykv[:, src])),
                      (kv_refs[g], GROUP_COLS + half * V7X_LANES, ykv[:, vsrc]))
            for out_ref, col, val in pieces:
                cols = slice(col, col + V7X_LANES)
                if dil == 1:
                    out_ref[0, :, cols] = val.astype(out_ref.dtype)
                else:
                    slab[slab_id] = val
                    for r in range(dil):
                        out_ref[r, :, cols] = slab[slab_id, pl.ds(r, n, stride=dil), :].astype(out_ref.dtype)
                    slab_id += 1


def kvq_proj(x, batch, seq_len, nkv, nq, wkv, wq, cos_t, sin_t, dils, tm):
    n, d = x.shape
    tps = seq_len // tm
    assert all(tm % dil == 0 and (tm // dil) % BF16_SUBLANES == 0 for dil in dils)
    n_slabs = max(1, 3 * (GROUP_COLS // V7X_LANES) * sum(dil > 1 for dil in dils))
    out_specs, out_shape = [], []
    for cols, dt in ((GROUP_COLS, BF16), (2 * GROUP_COLS, F32)):
        for dil in dils:
            out_specs.append(pl.BlockSpec((None, dil, tm // dil, cols), lambda b, i: (b, 0, i, 0)))
            out_shape.append(jax.ShapeDtypeStruct((batch, dil, seq_len // dil, cols), dt))
    return pl.pallas_call(
        functools.partial(_kvq_kernel, dils=dils),
        grid=(batch, tps),
        in_specs=[pl.BlockSpec((tm, d), lambda b, i: (b * tps + i, 0)), _const_spec((1, d)), _const_spec((1, d)),
                  _const_spec(wkv.shape), _const_spec(wq.shape),
                  pl.BlockSpec((tm, V7X_LANES), lambda b, i: (i, 0)),
                  pl.BlockSpec((tm, V7X_LANES), lambda b, i: (i, 0))],
        out_specs=out_specs,
        out_shape=out_shape,
        scratch_shapes=[pltpu.VMEM((n_slabs, tm, V7X_LANES), F32)],
        compiler_params=_params("parallel", "parallel"),
        name="kvq_proj",
    )(x, nkv, nq, wkv, wq, cos_t, sin_t)


def rope_tables(pos):
    half = ROT_DIM // 2
    inv_freq = ROPE_THETA ** (-jnp.arange(half, dtype=F32) / half)
    ang = pos.astype(F32)[:, None] * inv_freq[None, :]
    cos, sin = jnp.cos(ang), jnp.sin(ang)
    ones, zeros = jnp.ones_like(cos), jnp.zeros_like(sin)
    pad = (HEAD_DIM - ROT_DIM) // half
    cos_h = jnp.concatenate([cos, cos] + [ones] * pad, axis=1)
    sin_h = jnp.concatenate([-sin, sin] + [zeros] * pad, axis=1)
    reps = V7X_LANES // HEAD_DIM
    return jnp.tile(cos_h, (1, reps)), jnp.tile(sin_h, (1, reps))


def _head_masks(shape):
    lane = lax.broadcasted_iota(jnp.int32, shape, 1)
    return [(lane // HEAD_DIM) == h for h in range(HEADS_PER_GROUP)]


def _attn_prompt_kernel(q_ref, kvc_ref, kvp_ref, o_ref, l_ref, *, n_keys):
    i = pl.program_id(2)
    tq = n_keys
    n_sub = q_ref.shape[0] // tq
    kv = jnp.concatenate([kvp_ref[...], kvc_ref[...]], axis=0)
    k = kv[:, 0:GROUP_COLS].astype(BF16)
    v = kv[:, GROUP_COLS:2 * GROUP_COLS].astype(BF16)
    hm_q = _head_masks((tq, GROUP_COLS))
    hm_v = _head_masks((2 * tq, GROUP_COLS))
    qi = lax.broadcasted_iota(jnp.int32, (tq, 2 * tq), 0)
    kj = lax.broadcasted_iota(jnp.int32, (tq, 2 * tq), 1)
    rel = qi + tq - kj
    band = (rel >= 0) & (rel <= n_keys)
    bias = jnp.where(band, 0.0, -jnp.inf)
    bias_start = jnp.where(band & (kj >= tq), 0.0, -jnp.inf)
    bias_all = jnp.concatenate([bias] * HEADS_PER_GROUP, axis=0)
    bias_first = jnp.where(i == 0, jnp.concatenate([bias_start] * HEADS_PER_GROUP, axis=0), bias_all)
    for jb in range(n_sub):
        q = q_ref[jb * tq:(jb + 1) * tq, :]
        kb = k[jb * tq:(jb + 2) * tq]
        vb = v[jb * tq:(jb + 2) * tq]
        qs = jnp.concatenate([jnp.where(m, q, jnp.zeros_like(q)) for m in hm_q], axis=0)
        s = _dot_nt(qs, kb) * (HEAD_DIM ** -0.5) + (bias_first if jb == 0 else bias_all)
        m = jnp.max(s, axis=-1, keepdims=True)
        p = jnp.exp(s - m)
        l = jnp.sum(p, axis=-1, keepdims=True)
        pn = (p / l).astype(BF16)
        lse = m + jnp.log(l)
        o = jnp.zeros((tq, GROUP_COLS), F32)
        lmap = jnp.zeros((tq, GROUP_COLS), F32)
        for h in range(HEADS_PER_GROUP):
            rows = slice(h * tq, (h + 1) * tq)
            o = o + _dot(pn[rows], jnp.where(hm_v[h], vb, jnp.zeros_like(vb)))
            lmap = jnp.where(hm_q[h], lse[rows], lmap)
        o_ref[jb * tq:(jb + 1) * tq, :] = o
        l_ref[jb * tq:(jb + 1) * tq, :] = lmap


def attn_prompt(q, kv, group):
    window, dil = GROUPS[group]
    n_keys = window // dil
    batch, _, L, _ = q.shape
    tqb = _row_tile(L, ATTN_Q_ROWS)
    assert tqb % n_keys == 0
    per = tqb // n_keys
    cur = lambda b, r, i: (b, r, i, 0)
    prev = lambda b, r, i: (b, r, jnp.maximum(i * per - 1, 0), 0)
    return pl.pallas_call(
        functools.partial(_attn_prompt_kernel, n_keys=n_keys),
        grid=(batch, dil, L // tqb),
        in_specs=[pl.BlockSpec((None, None, tqb, GROUP_COLS), cur),
                  pl.BlockSpec((None, None, tqb, 2 * GROUP_COLS), cur),
                  pl.BlockSpec((None, None, n_keys, 2 * GROUP_COLS), prev)],
        out_specs=[pl.BlockSpec((None, None, tqb, GROUP_COLS), cur)] * 2,
        out_shape=[jax.ShapeDtypeStruct((batch, dil, L, GROUP_COLS), F32)] * 2,
        compiler_params=_params("parallel", "parallel", "parallel"),
        name=f"attn_prompt_g{group}",
    )(q, kv, kv)


def _attn_sample_kernel(*refs, n_new):
    q_refs, kvn_refs, c_refs = refs[0:N_GROUPS], refs[N_GROUPS:2 * N_GROUPS], refs[2 * N_GROUPS:3 * N_GROUPS]
    o_refs, l_refs = refs[3 * N_GROUPS:4 * N_GROUPS], refs[4 * N_GROUPS:5 * N_GROUPS]
    nq = q_refs[0].shape[0]
    n_stack = HEADS_PER_GROUP * nq
    scale = HEAD_DIM ** -0.5
    hm_q = _head_masks((nq, GROUP_COLS))
    t_col = lax.broadcasted_iota(jnp.int32, (n_stack, 1), 0) % nq
    for g, (window, dil) in enumerate(GROUPS):
        c_ref = c_refs[g]
        buf = c_ref.shape[1]
        kt = c_ref[0:GROUP_COLS, :].astype(BF16)
        vt = c_ref[GROUP_COLS:2 * GROUP_COLS, :].astype(BF16)
        kn = kvn_refs[g][:, 0:GROUP_COLS]
        vn = kvn_refs[g][:, GROUP_COLS:2 * GROUP_COLS]
        qf = q_refs[g][...].astype(F32)
        qsf = jnp.concatenate([jnp.where(hm, qf, 0.0) for hm in hm_q], axis=0)
        rel = buf + t_col - lax.broadcasted_iota(jnp.int32, (n_stack, buf), 1)
        s_c = _dot(qsf.astype(BF16), kt) * scale
        s_c = jnp.where(((rel & (dil - 1)) == 0) & (rel <= window), s_c, -jnp.inf)
        ok_n = [(t_col >= u) & (((t_col - u) & (dil - 1)) == 0) for u in range(n_new)]
        s_n = [jnp.where(ok_n[u], jnp.sum(qsf * kn[u:u + 1, :], axis=-1, keepdims=True) * scale, -jnp.inf)
               for u in range(n_new)]
        m = jnp.max(s_c, axis=-1, keepdims=True)
        for sn in s_n:
            m = jnp.maximum(m, sn)
        p_c = jnp.exp(s_c - m)
        p_n = [jnp.exp(sn - m) for sn in s_n]
        l = jnp.sum(p_c, axis=-1, keepdims=True)
        for pn in p_n:
            l = l + pn
        pc = p_c / l
        lse = m + jnp.log(l)
        feat_head = lax.broadcasted_iota(jnp.int32, vt.shape, 0) // HEAD_DIM
        o = jnp.zeros((nq, GROUP_COLS), F32)
        lmap = jnp.zeros((nq, GROUP_COLS), F32)
        for h in range(HEADS_PER_GROUP):
            rows = slice(h * nq, (h + 1) * nq)
            oh = _dot_nt(pc[rows].astype(BF16), jnp.where(feat_head == h, vt, jnp.zeros_like(vt)))
            for u in range(n_new):
                oh = oh + (p_n[u][rows] / l[rows]) * jnp.where(hm_q[h][0:1, :], vn[u:u + 1, :], 0.0)
            o = o + oh
            lmap = jnp.where(hm_q[h], lse[rows], lmap)
        o_refs[g][...] = o
        l_refs[g][...] = lmap


def attn_sample(qs, kv_news, caches, n_new):
    nb, nq, _ = qs[0].shape
    views = [c.reshape(nb, c.shape[1], 2 * GROUP_COLS).transpose(0, 2, 1) for c in caches]
    per_b = lambda rows, cols: pl.BlockSpec((None, rows, cols), lambda b: (b, 0, 0))
    in_specs = ([per_b(nq, GROUP_COLS)] * N_GROUPS + [per_b(n_new, 2 * GROUP_COLS)] * N_GROUPS
                + [per_b(2 * GROUP_COLS, v.shape[2]) for v in views])
    outs = pl.pallas_call(
        functools.partial(_attn_sample_kernel, n_new=n_new),
        grid=(nb,),
        in_specs=in_specs,
        out_specs=[per_b(nq, GROUP_COLS)] * (2 * N_GROUPS),
        out_shape=[jax.ShapeDtypeStruct((nb, nq, GROUP_COLS), F32)] * (2 * N_GROUPS),
        compiler_params=_params("parallel"),
        name="attn_sample",
    )(*qs, *kv_news, *views)
    return outs[0:N_GROUPS], outs[N_GROUPS:2 * N_GROUPS]


def _combine_wo_kernel(*refs, dils):
    o_refs, l_refs = refs[0:N_GROUPS], refs[N_GROUPS:2 * N_GROUPS]
    x_ref, w_ref, out_ref, slab = refs[2 * N_GROUPS:2 * N_GROUPS + 4]
    tm = x_ref.shape[0]
    slab_id = 0

    def in_position_order(ref, dil):
        nonlocal slab_id
        halves = []
        for half in range(GROUP_COLS // V7X_LANES):
            cols = slice(half * V7X_LANES, (half + 1) * V7X_LANES)
            if dil == 1:
                halves.append(ref[0, :, cols])
            else:
                for r in range(dil):
                    slab[slab_id, pl.ds(r, tm // dil, stride=dil), :] = ref[r, :, cols]
                halves.append(slab[slab_id])
                slab_id += 1
        return jnp.concatenate(halves, axis=1)

    ls = [in_position_order(l_refs[g], dils[g]) for g in range(N_GROUPS)]
    mx = jnp.maximum(jnp.maximum(ls[0], ls[1]), ls[2])
    es = [jnp.exp(l - mx) for l in ls]
    tot = es[0] + es[1] + es[2]
    acc = x_ref[...]
    for g in range(N_GROUPS):
        og = (in_position_order(o_refs[g], dils[g]) * (es[g] / tot)).astype(BF16)
        acc = acc + _dot(og, w_ref[g * GROUP_COLS:(g + 1) * GROUP_COLS, :])
    out_ref[...] = acc


def combine_wo(os_, ls, x, batch, seq_len, w_o, dils, tm):
    n, d = x.shape
    tps = seq_len // tm
    gspecs = [pl.BlockSpec((None, dil, tm // dil, GROUP_COLS), lambda b, i: (b, 0, i, 0)) for dil in dils]
    n_slabs = max(1, 2 * (GROUP_COLS // V7X_LANES) * sum(dil > 1 for dil in dils))
    return pl.pallas_call(
        functools.partial(_combine_wo_kernel, dils=dils),
        grid=(batch, tps),
        in_specs=gspecs + gspecs + [pl.BlockSpec((tm, d), lambda b, i: (b * tps + i, 0)), _const_spec(w_o.shape)],
        out_specs=pl.BlockSpec((tm, d), lambda b, i: (b * tps + i, 0)),
        out_shape=jax.ShapeDtypeStruct((n, d), F32),
        scratch_shapes=[pltpu.VMEM((n_slabs, tm, V7X_LANES), F32)],
        compiler_params=_params("parallel", "parallel"),
        name="combine_wo",
    )(*os_, *ls, x, w_o)


def _router_kernel(x_ref, nw_ref, wr_ref, xn_ref, meta_ref, metat_ref, cnt_ref, *, n_experts):
    xn = _rms(x_ref[...], nw_ref[...])
    xn_ref[...] = xn.astype(BF16)
    xh = xn.astype(BF16)
    xl = (xn - xh.astype(F32)).astype(BF16)
    wr = wr_ref[...]
    wh = wr.astype(BF16)
    wl = (wr - wh.astype(F32)).astype(BF16)
    logits = _dot(xh, wh) + (_dot(xl, wh) + _dot(xh, wl))
    lane = lax.broadcasted_iota(jnp.int32, logits.shape, 1)
    logits = jnp.where(lane < n_experts, logits, -jnp.inf)
    v1 = jnp.max(logits, axis=-1, keepdims=True)
    i1 = jnp.min(jnp.where(logits == v1, lane, V7X_LANES), axis=-1, keepdims=True)
    rest = jnp.where(lane == i1, -jnp.inf, logits)
    v2 = jnp.max(rest, axis=-1, keepdims=True)
    i2 = jnp.min(jnp.where(rest == v2, lane, V7X_LANES), axis=-1, keepdims=True)
    e2 = jnp.exp(v2 - v1)
    den = 1.0 + e2
    g1, g2 = 1.0 / den, e2 / den
    onehot = ((lane == i1) | (lane == i2)).astype(BF16)
    tm = onehot.shape[0]
    tri = (lax.broadcasted_iota(jnp.int32, (tm, tm), 0) >= lax.broadcasted_iota(jnp.int32, (tm, tm), 1)).astype(BF16)
    csum = _dot(tri, onehot)
    r1 = jnp.sum(jnp.where(lane == i1, csum, 0.0), axis=-1, keepdims=True) - 1.0
    r2 = jnp.sum(jnp.where(lane == i2, csum, 0.0), axis=-1, keepdims=True) - 1.0
    cnt_ref[...] = csum[tm - 1:tm, :].astype(jnp.int32)
    cols = (i1.astype(F32), i2.astype(F32), r1, r2, g1, g2)
    meta = jnp.zeros(logits.shape, F32)
    for ci, col in enumerate(cols):
        meta = jnp.where(lane == ci, col, meta)
    meta_ref[...] = meta[:, :META_COLS]
    metat_ref[...] = meta.T[:META_COLS, :]


def router(x, nw, w_router, tm):
    n, d = x.shape
    n_experts = w_router.shape[1]
    wr = jnp.pad(w_router, ((0, 0), (0, V7X_LANES - n_experts)))
    return pl.pallas_call(
        functools.partial(_router_kernel, n_experts=n_experts),
        grid=(n // tm,),
        in_specs=[pl.BlockSpec((tm, d), lambda i: (i, 0)), _const_spec((1, d)), _const_spec((d, V7X_LANES))],
        out_specs=[pl.BlockSpec((tm, d), lambda i: (i, 0)),
                   pl.BlockSpec((tm, META_COLS), lambda i: (i, 0)),
                   pl.BlockSpec((None, META_COLS, tm), lambda i: (i, 0, 0)),
                   pl.BlockSpec((None, 1, V7X_LANES), lambda i: (i, 0, 0))],
        out_shape=[jax.ShapeDtypeStruct((n, d), BF16),
                   jax.ShapeDtypeStruct((n, META_COLS), F32),
                   jax.ShapeDtypeStruct((n // tm, META_COLS, tm), F32),
                   jax.ShapeDtypeStruct((n // tm, 1, V7X_LANES), jnp.int32)],
        compiler_params=_params("parallel"),
        name="router",
    )(x, nw, wr)


def _segment_bits(tm):
    bits, b = [], SEG_ALIGN
    while b <= tm:
        bits.append(b)
        b *= 2
    return bits[::-1]


def _compact_rows(tm, n_experts):
    worst = TOP_K * tm + n_experts * (SEG_ALIGN - 1)
    return _round_up(worst, BF16_SUBLANES)


def _segment_copies(seg_sm, base_sm, tile, n_experts, tm, hbm_ref, vmem_ref, sems, to_hbm):
    out = []
    local = 0
    for e in range(n_experts):
        n = seg_sm[tile * n_experts + e]
        base = base_sm[tile * n_experts + e]
        for b, bit in enumerate(_segment_bits(tm)):
            done = n & ~(2 * bit - 1)
            src = vmem_ref.at[pl.ds(pl.multiple_of(local + done, SEG_ALIGN), bit)]
            dst = hbm_ref.at[pl.ds(pl.multiple_of(base + done, SEG_ALIGN), bit)]
            if not to_hbm:
                src, dst = dst, src
            out.append(((n & bit) != 0, pltpu.make_async_copy(src, dst, sems.at[e, b])))
        local = local + n
    return out, local


def _token_dest(e_k, r_k, seg_sm, tile, n_experts):
    dest = r_k
    local = 0
    for e in range(n_experts):
        dest = dest + jnp.where(e_k == e, local, 0)
        local = local + seg_sm[tile * n_experts + e]
    return dest


def _start_all(copies):
    for cond, cp in copies:
        pl.when(cond)(cp.start)


def _wait_all(copies):
    for cond, cp in copies:
        pl.when(cond)(cp.wait)


def _zero_copies(zn_sm, zbase_sm, n_spans, zero_rows, xs_hbm, zero_ref, zero_sems):
    out = []
    for k in range(n_spans):
        n, base = zn_sm[k], zbase_sm[k]
        for b, bit in enumerate(_segment_bits(zero_rows)):
            done = n & ~(2 * bit - 1)
            dst = xs_hbm.at[pl.ds(pl.multiple_of(base + done, SEG_ALIGN), bit)]
            out.append(((n & bit) != 0, pltpu.make_async_copy(zero_ref.at[pl.ds(0, bit)], dst, zero_sems.at[k, b])))
    return out


def _dispatch_kernel(seg_sm, base_sm, zn_sm, zbase_sm, tail_sm, xn_ref, metat_ref, *refs,
                     n_experts, n_tiles, tile0, n_spans, tmg, first_call):
    xs_hbm, comp_ref, zero_ref, sems, zero_sems, tail_sems = refs[-6:]
    i = pl.program_id(0)
    tile, slot = tile0 + i, i % 2
    tm, d = xn_ref.shape
    rows = comp_ref.shape[1]
    zero_rows = zero_ref.shape[0]

    def copies(t, s):
        return _segment_copies(seg_sm, base_sm, t, n_experts, tm, xs_hbm, comp_ref.at[s], sems.at[s], to_hbm=True)[0]

    if first_call:
        zeros = _zero_copies(zn_sm, zbase_sm, n_spans, zero_rows, xs_hbm, zero_ref, zero_sems)

        @pl.when(i == 0)
        def _():
            zero_ref[...] = jnp.zeros_like(zero_ref)
            _start_all(zeros)

    @pl.when(i < n_tiles)
    def _():
        mt = metat_ref[...]
        as_int = lambda v: v.astype(jnp.int32)
        dest1 = _token_dest(as_int(mt[0:1, :]), as_int(mt[2:3, :]), seg_sm, tile, n_experts)
        dest2 = _token_dest(as_int(mt[1:2, :]), as_int(mt[3:4, :]), seg_sm, tile, n_experts)
        row_id = lax.broadcasted_iota(jnp.int32, (rows, tm), 0)
        p1, p2 = row_id == dest1, row_id == dest2
        comp_ref[slot, :, 0:d] = _dot((p1 | p2).astype(BF16), xn_ref[...])
        gate = jnp.sum(jnp.where(p1, mt[4:5, :], 0.0) + jnp.where(p2, mt[5:6, :], 0.0), axis=-1, keepdims=True)
        comp_ref[slot, :, d:d + V7X_LANES] = jnp.broadcast_to(gate, (rows, V7X_LANES))
        _start_all(copies(tile, slot))

    @pl.when((i > 0) & (i <= n_tiles))
    def _():
        _wait_all(copies(tile - 1, 1 - slot))

    if first_call:
        @pl.when(i == n_tiles)
        def _():
            _wait_all(zeros)

        @pl.when((i > n_tiles) & (i - n_tiles - 1 < tail_sm[1]))
        def _():
            start = tail_sm[0] + (i - n_tiles - 1) * tmg
            parts = [pltpu.make_async_copy(
                zero_ref, xs_hbm.at[pl.ds(pl.multiple_of(start + p * zero_rows, SEG_ALIGN), zero_rows)], tail_sems.at[p])
                for p in range(tmg // zero_rows)]
            for cp in parts:
                cp.start()
            for cp in parts:
                cp.wait()


def dispatch(xn, metat, tables, n_experts, total_rows, tail_tiles, tm, tmg, tile0, xs_prev):
    n, d = xn.shape
    n_tiles = n // tm
    n_spans = tables[2].shape[0]
    rows = _compact_rows(tm, n_experts)
    zero_rows = tmg // 2
    first_call = xs_prev is None
    last_tile = lambda i: jnp.minimum(i, n_tiles - 1)
    in_specs = [pl.BlockSpec((tm, d), lambda i, *_: (last_tile(i), 0)),
                pl.BlockSpec((None, META_COLS, tm), lambda i, *_: (last_tile(i), 0, 0))]
    args = [*tables, xn, metat]
    if not first_call:
        in_specs.append(pl.BlockSpec(memory_space=pl.ANY))
        args.append(xs_prev)
    grid_spec = pltpu.PrefetchScalarGridSpec(
        num_scalar_prefetch=len(tables),
        grid=(n_tiles + 1 + (tail_tiles if first_call else 0),),
        in_specs=in_specs,
        out_specs=pl.BlockSpec(memory_space=pl.ANY),
        scratch_shapes=[pltpu.VMEM((2, rows, d + V7X_LANES), F32),
                        pltpu.VMEM((zero_rows, d + V7X_LANES), F32),
                        pltpu.SemaphoreType.DMA((2, n_experts, len(_segment_bits(tm)))),
                        pltpu.SemaphoreType.DMA((n_spans, len(_segment_bits(zero_rows)))),
                        pltpu.SemaphoreType.DMA((tmg // zero_rows,))],
    )
    return pl.pallas_call(
        functools.partial(_dispatch_kernel, n_experts=n_experts, n_tiles=n_tiles, tile0=tile0, n_spans=n_spans,
                          tmg=tmg, first_call=first_call),
        grid_spec=grid_spec,
        out_shape=jax.ShapeDtypeStruct((total_rows, d + V7X_LANES), F32),
        input_output_aliases={} if first_call else {len(args) - 1: 0},
        compiler_params=_params("arbitrary"),
        name="moe_dispatch",
    )(*args)


def _expert_ffn_kernel(te_sm, nu_sm, xs_ref, wg_ref, wu_ref, wd_ref, ys_ref, xb_ref, acc_ref):
    del te_sm
    j, c = pl.program_id(0), pl.program_id(1)
    last_c = pl.num_programs(1) - 1
    d = xb_ref.shape[1]
    used = j < nu_sm[0]

    @pl.when(used & (c == 0))
    def _():
        xb_ref[...] = xs_ref[:, 0:d].astype(BF16)
        acc_ref[...] = jnp.zeros_like(acc_ref)

    @pl.when(used)
    def _():
        xb = xb_ref[...]
        h = _silu(_dot(xb, wg_ref[...])) * _dot(xb, wu_ref[...])
        acc_ref[...] += _dot(h.astype(BF16), wd_ref[...])

    @pl.when(used & (c == last_c))
    def _():
        ys_ref[...] = xs_ref[:, d:d + 1] * acc_ref[...]

    @pl.when(jnp.logical_not(used) & (c == last_c))
    def _():
        ys_ref[...] = jnp.zeros_like(ys_ref)


def expert_ffn(xs, tile_expert, n_used, wg, wu, wd, tmg):
    total_rows, dx = xs.shape
    n_experts, d, de = wg.shape
    tf = next(c for c in (MOE_FF_CHUNK, MOE_FF_CHUNK // 2, _ff_chunk(de)) if de % c == 0)
    last_used = lambda j, nu: jnp.minimum(j, nu[0] - 1)
    grid_spec = pltpu.PrefetchScalarGridSpec(
        num_scalar_prefetch=2,
        grid=(total_rows // tmg, de // tf),
        in_specs=[pl.BlockSpec((tmg, dx), lambda j, c, te, nu: (last_used(j, nu), 0)),
                  pl.BlockSpec((None, d, tf), lambda j, c, te, nu: (te[j], 0, c)),
                  pl.BlockSpec((None, d, tf), lambda j, c, te, nu: (te[j], 0, c)),
                  pl.BlockSpec((None, tf, d), lambda j, c, te, nu: (te[j], c, 0))],
        out_specs=pl.BlockSpec((tmg, d), lambda j, c, te, nu: (j, 0)),
        scratch_shapes=[pltpu.VMEM((tmg, d), BF16), pltpu.VMEM((tmg, d), F32)],
    )
    return pl.pallas_call(
        _expert_ffn_kernel,
        grid_spec=grid_spec,
        out_shape=jax.ShapeDtypeStruct((total_rows, d), F32),
        compiler_params=_params("arbitrary", "arbitrary"),
        name="moe_expert_ffn",
    )(tile_expert, n_used, xs, wg, wu, wd)


def _combine_kernel(seg_sm, base_sm, x_ref, meta_ref, nf_ref, ys_hbm, y_ref, comp_ref, sems, *, n_experts, tile0):
    i, n_steps = pl.program_id(0), pl.num_programs(0)
    tile, slot = tile0 + i, i % 2
    tm, d = x_ref.shape
    rows = comp_ref.shape[1]

    def copies(t, s):
        return _segment_copies(seg_sm, base_sm, t, n_experts, tm, ys_hbm, comp_ref.at[s], sems.at[s], to_hbm=False)

    @pl.when(i == 0)
    def _():
        _start_all(copies(tile, slot)[0])

    @pl.when(i + 1 < n_steps)
    def _():
        _start_all(copies(tile + 1, 1 - slot)[0])

    meta = meta_ref[...]
    as_int = lambda v: v.astype(jnp.int32)
    dest1 = _token_dest(as_int(meta[:, 0:1]), as_int(meta[:, 2:3]), seg_sm, tile, n_experts)
    dest2 = _token_dest(as_int(meta[:, 1:2]), as_int(meta[:, 3:4]), seg_sm, tile, n_experts)
    col_id = lax.broadcasted_iota(jnp.int32, (tm, rows), 1)
    pick = ((col_id == dest1) | (col_id == dest2)).astype(BF16)
    mine, n_rows = copies(tile, slot)
    _wait_all(mine)
    live = lax.broadcasted_iota(jnp.int32, (rows, 1), 0) < n_rows
    ys = jnp.where(live, comp_ref[slot], 0.0)
    hi = ys.astype(BF16)
    lo = (ys - hi.astype(F32)).astype(BF16)
    y_ref[...] = _rms(x_ref[...] + (_dot(pick, hi) + _dot(pick, lo)), nf_ref[...])


def combine_final(x, meta, nf, ys, seg, base, n_experts, tm, tile0):
    n, d = x.shape
    rows = _compact_rows(tm, n_experts)
    grid_spec = pltpu.PrefetchScalarGridSpec(
        num_scalar_prefetch=2,
        grid=(n // tm,),
        in_specs=[pl.BlockSpec((tm, d), lambda i, *_: (i, 0)),
                  pl.BlockSpec((tm, META_COLS), lambda i, *_: (i, 0)),
                  pl.BlockSpec((1, d), lambda i, *_: (0, 0)),
                  pl.BlockSpec(memory_space=pl.ANY)],
        out_specs=pl.BlockSpec((tm, d), lambda i, *_: (i, 0)),
        scratch_shapes=[pltpu.VMEM((2, rows, d), F32),
                        pltpu.SemaphoreType.DMA((2, n_experts, len(_segment_bits(tm))))],
    )
    return pl.pallas_call(
        functools.partial(_combine_kernel, n_experts=n_experts, tile0=tile0),
        grid_spec=grid_spec,
        out_shape=jax.ShapeDtypeStruct((n, d), F32),
        compiler_params=_params("arbitrary"),
        name="moe_combine",
    )(seg, base, x, meta, nf, ys)


def moe_final(xs, tms, nw, nf, w_router, wg, wu, wd, tmg):
    n_experts = w_router.shape[1]
    routed = [router(x, nw, w_router, tm) for x, tm in zip(xs, tms)]
    n_tiles = [x.shape[0] // tm for x, tm in zip(xs, tms)]
    n_tokens = sum(x.shape[0] for x in xs)
    cnt = jnp.concatenate([r[3][:, 0, :n_experts] for r in routed], axis=0)
    seg = _round_up(cnt, SEG_ALIGN)
    within = jnp.cumsum(seg, axis=0) - seg
    exp_live = jnp.sum(seg, axis=0)
    exp_rows = _round_up(exp_live, tmg)
    exp_end = jnp.cumsum(exp_rows)
    base = (exp_end - exp_rows)[None, :] + within
    total_rows = _round_up(TOP_K * n_tokens + sum(n_tiles) * n_experts * (SEG_ALIGN - 1)
                           + n_experts * (tmg - SEG_ALIGN), tmg)
    n_used = (exp_end[-1] // tmg).reshape(1).astype(jnp.int32)
    tile_start = jnp.arange(total_rows // tmg, dtype=jnp.int32) * tmg
    tile_expert = jnp.minimum(jnp.sum(tile_start[:, None] >= exp_end[None, :], axis=1), n_experts - 1).astype(jnp.int32)
    i32 = lambda a: a.reshape(-1).astype(jnp.int32)
    zero_n = jnp.concatenate([exp_rows - exp_live, seg[n_tiles[0]:].reshape(-1)])
    zero_base = jnp.concatenate([exp_end - exp_rows + exp_live, base[n_tiles[0]:].reshape(-1)])
    tail = jnp.stack([exp_end[-1], (total_rows - exp_end[-1]) // tmg])
    tail_tiles = (total_rows - TOP_K * n_tokens) // tmg
    tables = (i32(seg), i32(base), i32(zero_n), i32(zero_base), i32(tail))
    rows_buf, tile0 = None, 0
    for (xn, _, metat, _), x, tm, nt in zip(routed, xs, tms, n_tiles):
        rows_buf = dispatch(xn, metat, tables, n_experts, total_rows, tail_tiles, tm, tmg, tile0, rows_buf)
        tile0 += nt
    ys = expert_ffn(rows_buf, tile_expert, n_used, wg, wu, wd, tmg)
    outs, tile0 = [], 0
    for (_, meta, _, _), x, tm, nt in zip(routed, xs, tms, n_tiles):
        outs.append(combine_final(x, meta, nf, ys, tables[0], tables[1], n_experts, tm, tile0))
        tile0 += nt
    return outs


def kernel(x_prompt, x_sample, state_conv, cache_kv_g0, cache_kv_g1, cache_kv_g2, norm_mix, norm_ffn, norm_kv, norm_final, w_pw1, b_pw1, w_dw, b_dw, ln_conv_g, ln_conv_b, w_pw2, b_pw2, w_q, w_kv, w_o, w_gate_dense, w_up_dense, w_down_dense, w_router, w_gate_exp, w_up_exp, w_down_exp):
    batch, seq_len, d = x_prompt.shape
    dec_batch, dec_seq, _ = x_sample.shape
    caches = (cache_kv_g0, cache_kv_g1, cache_kv_g2)
    assert norm_mix.shape[0] == 2 and state_conv.shape[0] == 1 and w_q.shape[0] == 1
    assert all(c.shape[1] == w for c, (w, _) in zip(caches, GROUPS))
    assert dec_seq <= SAMPLE_Q_ROWS

    row = lambda v: v.reshape(1, -1).astype(F32)
    bf = lambda w: w.astype(BF16)
    n_p = batch * seq_len
    n_s = dec_batch * dec_seq
    tm_p = _row_tile(seq_len, 512)
    tm_s = n_s
    dils = tuple(dil for _, dil in GROUPS)
    no_dil = (1,) * N_GROUPS

    wdw = jnp.pad(w_dw[0], ((0, TAP_ROWS - CONV_WIDTH), (0, 0)))
    conv_w = (wdw, row(b_dw[0]), row(ln_conv_g[0]), row(ln_conv_b[0]), bf(w_pw2[0]), row(b_pw2[0]))
    w_pw1_b, b_pw1_r = bf(w_pw1[0]), row(b_pw1[0])
    ffn_w = (bf(w_gate_dense[0]), bf(w_up_dense[0]), bf(w_down_dense[0]))
    kvq_w = (row(norm_kv), row(norm_mix[1]), bf(w_kv), bf(w_q[0]))
    w_o_b = bf(w_o[0])
    moe_w = (bf(w_gate_exp[0]), bf(w_up_exp[0]), bf(w_down_exp[0]))

    cos_p, sin_p = rope_tables(jnp.arange(seq_len))
    cos_s, sin_s = rope_tables(PAST_LEN + jnp.arange(n_s) // dec_batch)

    hp = x_prompt.reshape(n_p, d)
    hs = x_sample.transpose(1, 0, 2).reshape(n_s, d)
    g_p = pw1_glu(hp, row(norm_mix[0]), w_pw1_b, b_pw1_r, tm_p)
    g_s = pw1_glu(hs, row(norm_mix[0]), w_pw1_b, b_pw1_r, tm_s)
    wdw_rep = jnp.broadcast_to(w_dw[0][:, None, :], (CONV_WIDTH, V7X_SUBLANES, d))
    hp = conv_prompt(g_p, hp, seq_len, wdw_rep, *conv_w[1:], tm_p)
    full_s = jnp.concatenate([state_conv[0].transpose(1, 0, 2), g_s.reshape(dec_seq, dec_batch, d)], axis=0)
    hs = conv_sample(full_s, hs, *conv_w)
    conv_prompt_out = g_p.reshape(batch, seq_len, d)[:, seq_len - (CONV_WIDTH - 1):][None]
    conv_sample_out = full_s[dec_seq:].transpose(1, 0, 2)[None]

    hp = ffn(hp, row(norm_ffn[0]), *ffn_w, tm_p)
    hs = ffn(hs, row(norm_ffn[0]), *ffn_w, tm_s)

    kvq_p = kvq_proj(hp, batch, seq_len, *kvq_w, cos_p, sin_p, dils, tm_p)
    q_p, kv_p = kvq_p[0:N_GROUPS], kvq_p[N_GROUPS:]
    kvq_s = kvq_proj(hs, 1, n_s, *kvq_w, cos_s, sin_s, no_dil, tm_s)
    to_bm = lambda a: a.reshape(dec_seq, dec_batch, -1).transpose(1, 0, 2)
    q_s = [jnp.pad(to_bm(a), ((0, 0), (0, SAMPLE_Q_ROWS - dec_seq), (0, 0))) for a in kvq_s[0:N_GROUPS]]
    kv_s = [to_bm(a) for a in kvq_s[N_GROUPS:]]

    att_p = [attn_prompt(q_p[g], kv_p[g], g) for g in range(N_GROUPS)]
    hp = combine_wo([a[0] for a in att_p], [a[1] for a in att_p], hp, batch, seq_len, w_o_b, dils, tm_p)

    hs = to_bm(hs).reshape(n_s, d)
    o_s, l_s = attn_sample(q_s, kv_s, caches, dec_seq)
    as_rows = lambda a: a[:, :dec_seq].reshape(1, 1, n_s, GROUP_COLS)
    hs = combine_wo([as_rows(a) for a in o_s], [as_rows(a) for a in l_s], hs, 1, n_s, w_o_b, no_dil, tm_s)

    y_p, y_s = moe_final([hp, hs], [tm_p, tm_s], row(norm_ffn[1]), row(norm_final), w_router[0], *moe_w,
                         MOE_ROW_TILE)

    kv_out = []
    for g, (window, dil) in enumerate(GROUPS):
        keep = min(window, seq_len)
        tail = kv_p[g][:, :, (seq_len - keep) // dil:, :]
        kv_out.append(tail.transpose(0, 2, 1, 3).reshape(batch, keep, 2, HEADS_PER_GROUP, HEAD_DIM))
        kv_out.append(kv_s[g].reshape(dec_batch, dec_seq, 2, HEADS_PER_GROUP, HEAD_DIM))
    return (y_p.reshape(batch, seq_len, d), y_s.reshape(dec_batch, dec_seq, d),
            conv_prompt_out, conv_sample_out, *kv_out)
```
